```python
import jax, jax.numpy as jnp
from jax import lax
import numpy as np

D_MODEL = 1024
BATCH = 2
SEQ = 8192
DEPTH = 1

CHUNK = 64
Q_BLOCK = 128
MIX_WIDTH = D_MODEL
FOX_HEAD_DIM = 64
FOX_WIDTH = MIX_WIDTH // 2
FOX_HEADS = FOX_WIDTH // FOX_HEAD_DIM
GLA_HEADS = 4
GLA_VAL_WIDTH = MIX_WIDTH - FOX_WIDTH
GLA_VAL_DIM = GLA_VAL_WIDTH // GLA_HEADS
GLA_KEY_WIDTH = GLA_VAL_WIDTH // 2
GLA_KEY_DIM = GLA_KEY_WIDTH // GLA_HEADS
GLA_GATE_RANK = 16
GLA_GATE_TAU = 16.0
N_EXPERTS = 32
TOP_K = 4
D_FF = D_MODEL
SWIGLU_LIMIT = 7.0
SWIGLU_ALPHA = 1.702
EPS = 1e-5

PROJ_SPLITS = (FOX_WIDTH, FOX_WIDTH, FOX_WIDTH, FOX_HEADS,
               GLA_KEY_WIDTH, GLA_KEY_WIDTH, GLA_VAL_WIDTH, GLA_GATE_RANK, GLA_VAL_WIDTH)
PROJ_WIDTH = sum(PROJ_SPLITS)

kernel_name = "hybrid_fox_gla_moe_block"


def rmsnorm(x, g):
    xf = x.astype(jnp.float32)
    y = xf * lax.rsqrt(jnp.mean(xf * xf, axis=-1, keepdims=True) + EPS)
    return (y * g.astype(jnp.float32)).astype(x.dtype)


def split_columns(p):
    idx, acc = [], 0
    for s in PROJ_SPLITS[:-1]:
        acc += s
        idx.append(acc)
    return jnp.split(p, idx, axis=-1)


def forgetting_attention(q, k, v, f_logit):
    B, S, H, Dh = q.shape
    n_blocks = S // Q_BLOCK
    cum = jnp.cumsum(jax.nn.log_sigmoid(f_logit.astype(jnp.float32)), axis=1)
    cum_h = cum.transpose(0, 2, 1)
    qb = q.reshape(B, n_blocks, Q_BLOCK, H, Dh).transpose(1, 0, 3, 2, 4)
    cq = cum_h.reshape(B, H, n_blocks, Q_BLOCK).transpose(2, 0, 1, 3)
    kh = k.transpose(0, 2, 1, 3)
    vh = v.transpose(0, 2, 1, 3)
    key_pos = jnp.arange(S)
    scale = Dh ** -0.5

    def block(args):
        qi, cqi, i = args
        s = jnp.einsum('bhqd,bhkd->bhqk', qi, kh, preferred_element_type=jnp.float32) * scale
        s = s + cqi[..., :, None] - cum_h[..., None, :]
        q_pos = i * Q_BLOCK + jnp.arange(Q_BLOCK)
        s = jnp.where(key_pos[None, :] <= q_pos[:, None], s, -jnp.inf)
        p = jax.nn.softmax(s, axis=-1)
        return jnp.einsum('bhqk,bhkd->bhqd', p.astype(vh.dtype), vh)

    out = lax.map(block, (qb, cq, jnp.arange(n_blocks)))
    return out.transpose(1, 0, 3, 2, 4).reshape(B, S, H * Dh)


def gla_chunk_causal(q, k, v, log_a):
    B, S, H, Dk = q.shape
    Dv = v.shape[-1]
    nc = S // CHUNK
    qc = q.astype(jnp.float32).reshape(B, nc, CHUNK, H, Dk) * (Dk ** -0.5)
    kc = k.astype(jnp.float32).reshape(B, nc, CHUNK, H, Dk)
    vc = v.astype(jnp.float32).reshape(B, nc, CHUNK, H, Dv)
    cum = jnp.cumsum(log_a.astype(jnp.float32).reshape(B, nc, CHUNK, H, Dk), axis=2)
    total = cum[:, :, -1]
    k_dec = kc * jnp.exp(total[:, :, None] - cum)
    upd = jnp.einsum('bcshk,bcshv->bchkv', k_dec, vc)
    decay = jnp.exp(total)

    def step(state, inp):
        d, u = inp
        state = d[..., None] * state + u
        return state, state

    init = jnp.zeros((B, H, Dk, Dv), jnp.float32)
    _, states = lax.scan(step, init, (decay.transpose(1, 0, 2, 3), upd.transpose(1, 0, 2, 3, 4)))
    states = states.transpose(1, 0, 2, 3, 4)
    o = jnp.einsum('bcthk,bchkv->bcthv', qc, states)
    return o.reshape(B, S, H, Dv)


def hybrid_mixer(n, w_in, fox_f_bias, gla_gate_up, gla_gate_bias, gla_norm_g, w_out):
    B, S, _ = n.shape
    proj = n @ w_in
    fq, fk, fv, ff, gq, gk, gv, g_low, g_out = split_columns(proj)
    fox = forgetting_attention(fq.reshape(B, S, FOX_HEADS, FOX_HEAD_DIM),
                               fk.reshape(B, S, FOX_HEADS, FOX_HEAD_DIM),
                               fv.reshape(B, S, FOX_HEADS, FOX_HEAD_DIM),
                               ff + fox_f_bias)
    log_a = jax.nn.log_sigmoid((g_low @ gla_gate_up + gla_gate_bias).astype(jnp.float32)) / GLA_GATE_TAU
    o = gla_chunk_causal(gq.reshape(B, S, GLA_HEADS, GLA_KEY_DIM),
                         gk.reshape(B, S, GLA_HEADS, GLA_KEY_DIM),
                         gv.reshape(B, S, GLA_HEADS, GLA_VAL_DIM),
                         log_a.reshape(B, S, GLA_HEADS, GLA_KEY_DIM))
    o = o * lax.rsqrt(jnp.mean(o * o, axis=-1, keepdims=True) + EPS) * gla_norm_g.astype(jnp.float32)
    gla = (o.reshape(B, S, GLA_VAL_WIDTH) * jax.nn.silu(g_out.astype(jnp.float32))).astype(n.dtype)
    return jnp.concatenate([fox, gla], axis=-1) @ w_out


def moe_ffn(n, router_w, router_b, exp_w_in, exp_b_in, exp_w_out, exp_b_out):
    B, S, D = n.shape
    t = n.reshape(-1, D)
    logits = (t @ router_w + router_b).astype(jnp.float32)
    top_v, top_i = lax.top_k(logits, TOP_K)
    gates = jax.nn.softmax(top_v, axis=-1)
    combine = jnp.sum(jax.nn.one_hot(top_i, N_EXPERTS, dtype=jnp.float32) * gates[..., None], axis=1)

    def expert(acc, p):
        wi, bi, wo, bo, c = p
        h = t @ wi + bi
        gate = jnp.minimum(h[:, :D_FF], SWIGLU_LIMIT)
        lin = jnp.clip(h[:, D_FF:], -SWIGLU_LIMIT, SWIGLU_LIMIT)
        y = ((lin + 1.0) * (gate * jax.nn.sigmoid(SWIGLU_ALPHA * gate))) @ wo + bo
        return acc + c[:, None] * y.astype(jnp.float32), None

    acc, _ = lax.scan(expert, jnp.zeros(t.shape, jnp.float32),
                      (exp_w_in, exp_b_in, exp_w_out, exp_b_out, combine.T))
    return acc.astype(n.dtype).reshape(B, S, D)


def setup_inputs(seed: int = 0) -> dict:
    key = jax.random.key(seed)
    ks = jax.random.split(key, 20)
    f32 = jnp.float32
    nrm = lambda k, shape, s: jax.random.normal(k, shape, f32) * s
    return {
        "x": jax.random.normal(ks[0], (BATCH, SEQ, D_MODEL), f32),
        "norm1_g": 1.0 + nrm(ks[1], (DEPTH, D_MODEL), 0.02),
        "w_in": nrm(ks[2], (DEPTH, D_MODEL, PROJ_WIDTH), D_MODEL ** -0.5),
        "fox_f_bias": jnp.linspace(1.0, 5.0, FOX_HEADS, dtype=f32)[None] + nrm(ks[3], (DEPTH, FOX_HEADS), 0.1),
        "gla_gate_up": nrm(ks[4], (DEPTH, GLA_GATE_RANK, GLA_KEY_WIDTH), GLA_GATE_RANK ** -0.5),
        "gla_gate_bias": nrm(ks[5], (DEPTH, GLA_KEY_WIDTH), 0.1),
        "gla_norm_g": 1.0 + nrm(ks[6], (DEPTH, GLA_VAL_DIM), 0.02),
        "w_out": nrm(ks[7], (DEPTH, MIX_WIDTH, D_MODEL), MIX_WIDTH ** -0.5),
        "norm2_g": 1.0 + nrm(ks[8], (DEPTH, D_MODEL), 0.02),
        "router_w": nrm(ks[9], (DEPTH, D_MODEL, N_EXPERTS), D_MODEL ** -0.5),
        "router_b": nrm(ks[10], (DEPTH, N_EXPERTS), 0.01),
        "exp_w_in": nrm(ks[11], (DEPTH, N_EXPERTS, D_MODEL, 2 * D_FF), D_MODEL ** -0.5),
        "exp_b_in": nrm(ks[12], (DEPTH, N_EXPERTS, 2 * D_FF), 0.02),
        "exp_w_out": nrm(ks[13], (DEPTH, N_EXPERTS, D_FF, D_MODEL), D_FF ** -0.5),
        "exp_b_out": nrm(ks[14], (DEPTH, N_EXPERTS, D_MODEL), 0.02),
        "final_g": 1.0 + nrm(ks[15], (D_MODEL,), 0.02),
    }


def reference(x, norm1_g, w_in, fox_f_bias, gla_gate_up, gla_gate_bias, gla_norm_g, w_out,
              norm2_g, router_w, router_b, exp_w_in, exp_b_in, exp_w_out, exp_b_out, final_g):
    h = x
    for l in range(DEPTH):
        n = rmsnorm(h, norm1_g[l])
        h = h + hybrid_mixer(n, w_in[l], fox_f_bias[l], gla_gate_up[l], gla_gate_bias[l],
                             gla_norm_g[l], w_out[l])
        n = rmsnorm(h, norm2_g[l])
        h = h + moe_ffn(n, router_w[l], router_b[l], exp_w_in[l], exp_b_in[l],
                        exp_w_out[l], exp_b_out[l])
    return rmsnorm(h, final_g)
```

```python
import functools

import jax
import jax.numpy as jnp
from jax import lax
from jax.experimental import pallas as pl
from jax.experimental.pallas import tpu as pltpu

F32 = jnp.float32
BF16 = jnp.bfloat16

D_MODEL = 1024
FOX_HEADS = 8
FOX_HEAD_DIM = 64
FOX_WIDTH = 512
GLA_HEADS = 4
GLA_KEY_DIM = 64
GLA_KEY_WIDTH = 256
GLA_VAL_DIM = 128
GLA_VAL_WIDTH = 512
GLA_GATE_RANK = 16
GLA_GATE_TAU = 16.0
CHUNK = 64
N_EXPERTS = 32
TOP_K = 4
D_FF = 1024
SWIGLU_LIMIT = 7.0
SWIGLU_ALPHA = 1.702
EPS = 1e-5

LANES = 128
SUBLANES = 8
ROW_TILES = D_MODEL // LANES
VMEM_LIMIT = 56 * 1024 * 1024

TM_PROJ = 256
TQ = 256
GLA_ROWS = 512
TM_EXP = 256
TB_DISPATCH = 512
TC_COMBINE = 256
MAIN_WIDTH = 3 * FOX_WIDTH + 2 * GLA_KEY_WIDTH + 2 * GLA_VAL_WIDTH
NEG_BIG = -1e30


def _log_sigmoid(z):
    return jnp.minimum(z, 0.0) - jnp.log1p(jnp.exp(-jnp.abs(z)))


def _split3(a):
    p1 = a.astype(BF16)
    r1 = a - p1.astype(F32)
    p2 = r1.astype(BF16)
    r2 = r1 - p2.astype(F32)
    return p1, p2, r2.astype(BF16)


def _dot(a, b):
    return jnp.dot(a, b, preferred_element_type=F32)


def _dot_nt(a, b):
    return lax.dot_general(a, b, (((1,), (1,)), ((), ())), preferred_element_type=F32)


def _dot_tn(a, b):
    return lax.dot_general(a, b, (((0,), (0,)), ((), ())), preferred_element_type=F32)


def _dot3_exact_lhs(tri, a):
    p1, p2, p3 = _split3(a)
    return _dot(tri, p1) + _dot(tri, p2) + _dot(tri, p3)


def _dot_hi(a, b_hi, b_lo):
    a_hi = a.astype(BF16)
    a_lo = (a - a_hi.astype(F32)).astype(BF16)
    return _dot(a_hi, b_hi) + _dot(a_lo, b_hi) + _dot(a_hi, b_lo)


def _rms(x, g):
    return x * lax.rsqrt(jnp.mean(x * x, axis=-1, keepdims=True) + EPS) * g


def _in_proj_kernel(x_ref, g_ref, wm_ref, wsh_ref, wsl_ref, fb_ref, guh_ref, gul_ref, gb_ref,
                    fq_ref, fk_ref, fv_ref, gq_ref, gk_ref, gv_ref, go_ref, la_ref, f_ref, ft_ref,
                    carry_ref):
    tm = x_ref.shape[0]

    @pl.when(pl.program_id(1) == 0)
    def _():
        carry_ref[...] = jnp.zeros_like(carry_ref)

    n = _rms(x_ref[...], g_ref[...])
    main = _dot(n.astype(BF16), wm_ref[...])
    o = 0
    fq_ref[...] = (main[:, o:o + FOX_WIDTH] * (FOX_HEAD_DIM ** -0.5)).astype(BF16); o += FOX_WIDTH
    fk_ref[...] = main[:, o:o + FOX_WIDTH].astype(BF16); o += FOX_WIDTH
    fv_ref[...] = main[:, o:o + FOX_WIDTH].astype(BF16); o += FOX_WIDTH
    gq_ref[...] = (main[:, o:o + GLA_KEY_WIDTH] * (GLA_KEY_DIM ** -0.5)).astype(BF16); o += GLA_KEY_WIDTH
    gk_ref[...] = main[:, o:o + GLA_KEY_WIDTH]; o += GLA_KEY_WIDTH
    gv_ref[...] = main[:, o:o + GLA_VAL_WIDTH].astype(BF16); o += GLA_VAL_WIDTH
    go_ref[...] = main[:, o:o + GLA_VAL_WIDTH]

    small = _dot_hi(n, wsh_ref[...], wsl_ref[...])

    ls = _log_sigmoid(small + fb_ref[...])
    row = lax.broadcasted_iota(jnp.int32, (tm, tm), 0)
    col = lax.broadcasted_iota(jnp.int32, (tm, tm), 1)
    tri = jnp.where(row >= col, 1.0, 0.0).astype(BF16)
    cum = _dot3_exact_lhs(tri, ls) + carry_ref[...]
    carry_ref[...] = cum[tm - 1:tm, :]
    f_ref[...] = cum
    ft_ref[...] = cum.T[0:FOX_HEADS, :]

    z = _dot_hi(small, guh_ref[...], gul_ref[...]) + gb_ref[...]
    la_ref[...] = _log_sigmoid(z) * (1.0 / GLA_GATE_TAU)


def _in_proj(x2, norm_g, w_main, ws_hi, ws_lo, fb_pad, gu_hi, gu_lo, gb, batch, seq):
    t = batch * seq
    tm = TM_PROJ
    nj = seq // tm
    rows = lambda w: pl.BlockSpec((tm, w), lambda b, j: (b * nj + j, 0))
    full = lambda a: pl.BlockSpec(a.shape, lambda b, j: (0,) * a.ndim)
    out_shape = [
        jax.ShapeDtypeStruct((t, FOX_WIDTH), BF16),
        jax.ShapeDtypeStruct((t, FOX_WIDTH), BF16),
        jax.ShapeDtypeStruct((t, FOX_WIDTH), BF16),
        jax.ShapeDtypeStruct((t, GLA_KEY_WIDTH), BF16),
        jax.ShapeDtypeStruct((t, GLA_KEY_WIDTH), F32),
        jax.ShapeDtypeStruct((t, GLA_VAL_WIDTH), BF16),
        jax.ShapeDtypeStruct((t, GLA_VAL_WIDTH), F32),
        jax.ShapeDtypeStruct((t, GLA_KEY_WIDTH), F32),
        jax.ShapeDtypeStruct((t, LANES), F32),
        jax.ShapeDtypeStruct((batch, FOX_HEADS, seq), F32),
    ]
    out_specs = [rows(FOX_WIDTH), rows(FOX_WIDTH), rows(FOX_WIDTH), rows(GLA_KEY_WIDTH),
                 rows(GLA_KEY_WIDTH), rows(GLA_VAL_WIDTH), rows(GLA_VAL_WIDTH), rows(GLA_KEY_WIDTH),
                 rows(LANES),
                 pl.BlockSpec((None, FOX_HEADS, tm), lambda b, j: (b, 0, j))]
    return pl.pallas_call(
        _in_proj_kernel,
        grid=(batch, nj),
        in_specs=[rows(D_MODEL), full(norm_g), full(w_main), full(ws_hi), full(ws_lo), full(fb_pad),
                  full(gu_hi), full(gu_lo), full(gb)],
        out_specs=out_specs,
        out_shape=out_shape,
        scratch_shapes=[pltpu.VMEM((1, LANES), F32)],
        compiler_params=pltpu.CompilerParams(
            dimension_semantics=("arbitrary", "arbitrary"), vmem_limit_bytes=VMEM_LIMIT),
        name="in_proj",
    )(x2, norm_g, w_main, ws_hi, ws_lo, fb_pad, gu_hi, gu_lo, gb)


def _fox_kernel(q_ref, k_ref, v_ref, fq_ref, ft_ref, o_ref, acc_ref, m_ref, l_ref):
    tq = q_ref.shape[0]
    i = pl.program_id(2)
    lane = lax.broadcasted_iota(jnp.int32, (tq, LANES), 1)
    q = q_ref[...]
    zero = jnp.zeros_like(q)
    q_heads = (jnp.where(lane < FOX_HEAD_DIM, q, zero), jnp.where(lane >= FOX_HEAD_DIM, q, zero))
    fq = fq_ref[...]
    fq_heads = (fq[:, 0:1], fq[:, 1:2])

    m_ref[...] = jnp.full_like(m_ref, -jnp.inf)
    l_ref[...] = jnp.zeros_like(l_ref)
    acc_ref[...] = jnp.zeros_like(acc_ref)

    def step(j, masked):
        start = pl.multiple_of(j * tq, tq)
        k = k_ref[pl.ds(start, tq), :]
        v = v_ref[pl.ds(start, tq), :]
        for h in range(2):
            s = _dot_nt(q_heads[h], k)
            s = s + (fq_heads[h] - ft_ref[h:h + 1, pl.ds(start, tq)])
            if masked:
                r = lax.broadcasted_iota(jnp.int32, (tq, tq), 0)
                c = lax.broadcasted_iota(jnp.int32, (tq, tq), 1)
                s = jnp.where(c <= r, s, -jnp.inf)
            m_prev = m_ref[h]
            m_new = jnp.maximum(m_prev, jnp.max(s, axis=-1, keepdims=True))
            p = jnp.exp(s - m_new)
            alpha = jnp.exp(m_prev - m_new)
            l_ref[h] = alpha * l_ref[h] + jnp.sum(p, axis=-1, keepdims=True)
            acc_ref[h] = alpha * acc_ref[h] + _dot(p.astype(BF16), v)
            m_ref[h] = m_new

    def body(j, c):
        step(j, False)
        return c

    lax.fori_loop(0, i, body, 0)
    step(i, True)
    o = jnp.where(lane < FOX_HEAD_DIM, acc_ref[0] / l_ref[0], acc_ref[1] / l_ref[1])
    o_ref[...] = o.astype(o_ref.dtype)


def _fox(fq, fk, fv, f_pairs, ft_pairs, batch, seq):
    t = batch * seq
    nq = seq // TQ
    pairs = FOX_HEADS // 2
    return pl.pallas_call(
        _fox_kernel,
        grid=(batch, pairs, nq),
        in_specs=[
            pl.BlockSpec((TQ, LANES), lambda b, p, i: (b * nq + i, p)),
            pl.BlockSpec((seq, LANES), lambda b, p, i: (b, p)),
            pl.BlockSpec((seq, LANES), lambda b, p, i: (b, p)),
            pl.BlockSpec((None, None, TQ, 2), lambda b, p, i: (b, p, i, 0)),
            pl.BlockSpec((None, None, 2, seq), lambda b, p, i: (b, p, 0, 0)),
        ],
        out_specs=pl.BlockSpec((TQ, LANES), lambda b, p, i: (b * nq + i, p)),
        out_shape=jax.ShapeDtypeStruct((t, FOX_WIDTH), BF16),
        scratch_shapes=[pltpu.VMEM((2, TQ, LANES), F32), pltpu.VMEM((2, TQ, 1), F32),
                        pltpu.VMEM((2, TQ, 1), F32)],
        compiler_params=pltpu.CompilerParams(
            dimension_semantics=("parallel", "parallel", "arbitrary"), vmem_limit_bytes=VMEM_LIMIT),
        name="fox_attention",
    )(fq, fk, fv, f_pairs, ft_pairs)


def _gla_kernel(q_ref, k_ref, la_ref, v_ref, go_ref, ng_ref, o_ref, st_ref, obuf_ref):
    rows = q_ref.shape[0]

    @pl.when(pl.program_id(2) == 0)
    def _():
        st_ref[...] = jnp.zeros_like(st_ref)

    r = lax.broadcasted_iota(jnp.int32, (CHUNK, CHUNK), 0)
    c = lax.broadcasted_iota(jnp.int32, (CHUNK, CHUNK), 1)
    tri = jnp.where(r >= c, 1.0, 0.0).astype(BF16)
    vrow = lax.broadcasted_iota(jnp.int32, (2 * GLA_VAL_DIM, 2 * GLA_KEY_DIM), 0)
    kcol = lax.broadcasted_iota(jnp.int32, (2 * GLA_VAL_DIM, 2 * GLA_KEY_DIM), 1)
    same_head = (vrow >= GLA_VAL_DIM) == (kcol >= GLA_KEY_DIM)

    for ch in range(rows // CHUNK):
        sl = slice(ch * CHUNK, (ch + 1) * CHUNK)
        cum = _dot3_exact_lhs(tri, la_ref[sl, :])
        tot = cum[CHUNK - 1:CHUNK, :]
        kd = (k_ref[sl, :] * jnp.exp(tot - cum)).astype(BF16)
        upd_t = _dot_tn(v_ref[sl, :], kd)
        st = st_ref[...] * jnp.exp(tot) + jnp.where(same_head, upd_t, 0.0)
        st_ref[...] = st
        obuf_ref[sl, :] = _dot_nt(q_ref[sl, :], st.astype(BF16))

    o = obuf_ref[...]
    go = go_ref[...]
    ng = ng_ref[...]
    halves = []
    for h in range(2):
        hs = slice(h * GLA_VAL_DIM, (h + 1) * GLA_VAL_DIM)
        oh = o[:, hs]
        oh = oh * lax.rsqrt(jnp.mean(oh * oh, axis=-1, keepdims=True) + EPS) * ng
        gh = go[:, hs]
        halves.append(oh * (gh * jax.nn.sigmoid(gh)))
    o_ref[...] = jnp.concatenate(halves, axis=1).astype(o_ref.dtype)


def _gla(gq, gk, la, gv, go, ng, batch, seq):
    t = batch * seq
    nr = seq // GLA_ROWS
    pairs = GLA_HEADS // 2
    kspec = pl.BlockSpec((GLA_ROWS, 2 * GLA_KEY_DIM), lambda b, p, i: (b * nr + i, p))
    vspec = pl.BlockSpec((GLA_ROWS, 2 * GLA_VAL_DIM), lambda b, p, i: (b * nr + i, p))
    return pl.pallas_call(
        _gla_kernel,
        grid=(batch, pairs, nr),
        in_specs=[kspec, kspec, kspec, vspec, vspec,
                  pl.BlockSpec((1, GLA_VAL_DIM), lambda b, p, i: (0, 0))],
        out_specs=vspec,
        out_shape=jax.ShapeDtypeStruct((t, GLA_VAL_WIDTH), BF16),
        scratch_shapes=[pltpu.VMEM((2 * GLA_VAL_DIM, 2 * GLA_KEY_DIM), F32),
                        pltpu.VMEM((GLA_ROWS, 2 * GLA_VAL_DIM), F32)],
        compiler_params=pltpu.CompilerParams(
            dimension_semantics=("parallel", "parallel", "arbitrary"), vmem_limit_bytes=VMEM_LIMIT),
        name="gla",
    )(gq, gk, la, gv, go, ng)


def _out_proj_kernel(x_ref, fox_ref, gla_ref, wf_ref, wg_ref, g2_ref, rwh_ref, rwl_ref, rb_ref,
                     h_ref, n2_ref, ti_ref, gate_ref, rank_ref, cnt_ref, carry_ref):
    tm = x_ref.shape[0]

    @pl.when(pl.program_id(0) == 0)
    def _():
        carry_ref[...] = jnp.zeros_like(carry_ref)

    h = x_ref[...] + _dot(fox_ref[...], wf_ref[...]) + _dot(gla_ref[...], wg_ref[...])
    h_ref[...] = h
    n2 = _rms(h, g2_ref[...])
    for c in range(ROW_TILES):
        n2_ref[pl.ds(c, tm, stride=ROW_TILES), :] = n2[:, c * LANES:(c + 1) * LANES]

    logits = _dot_hi(n2, rwh_ref[...], rwl_ref[...]) + rb_ref[...]
    lane = lax.broadcasted_iota(jnp.int32, (tm, LANES), 1)
    work = logits
    vals, idxs = [], []
    for _ in range(TOP_K):
        m = jnp.max(work, axis=-1, keepdims=True)
        idx = jnp.min(jnp.where(work == m, lane, LANES), axis=-1, keepdims=True)
        vals.append(m)
        idxs.append(idx)
        work = jnp.where(lane == idx, -jnp.inf, work)
    exps = [jnp.exp(v - vals[0]) for v in vals]
    denom = exps[0] + exps[1] + exps[2] + exps[3]

    onehots = [lane == idx for idx in idxs]
    chosen = jnp.zeros((tm, LANES), F32)
    for oh in onehots:
        chosen = chosen + jnp.where(oh, 1.0, 0.0)
    row = lax.broadcasted_iota(jnp.int32, (tm, tm), 0)
    col = lax.broadcasted_iota(jnp.int32, (tm, tm), 1)
    strict = jnp.where(row > col, 1.0, 0.0).astype(BF16)
    before = _dot(strict, chosen.astype(BF16)) + carry_ref[...]
    carry = carry_ref[...] + jnp.sum(chosen, axis=0, keepdims=True)
    carry_ref[...] = carry
    cnt_ref[...] = jnp.broadcast_to(carry, cnt_ref.shape).astype(jnp.int32)

    ti = jnp.zeros((tm, LANES), jnp.int32)
    gates = jnp.zeros((tm, LANES), F32)
    ranks = jnp.zeros((tm, LANES), F32)
    for kk in range(TOP_K):
        sel = lane == kk
        ti = jnp.where(sel, idxs[kk], ti)
        gates = jnp.where(sel, exps[kk] / denom, gates)
        rk = jnp.sum(jnp.where(onehots[kk], before, 0.0), axis=-1, keepdims=True)
        ranks = jnp.where(sel, rk, ranks)
    ti_ref[...] = ti
    gate_ref[...] = gates
    rank_ref[...] = ranks.astype(jnp.int32)


def _out_proj(x2, fox, gla, wf, wg, g2, rw_hi, rw_lo, rb_pad):
    t = x2.shape[0]
    tm = TM_PROJ
    rows = lambda w: pl.BlockSpec((tm, w), lambda i: (i, 0))
    full = lambda a: pl.BlockSpec(a.shape, lambda i: (0,) * a.ndim)
    out_shape = [
        jax.ShapeDtypeStruct((t, D_MODEL), F32),
        jax.ShapeDtypeStruct((t * ROW_TILES, LANES), F32),
        jax.ShapeDtypeStruct((t, LANES), jnp.int32),
        jax.ShapeDtypeStruct((t, LANES), F32),
        jax.ShapeDtypeStruct((t, LANES), jnp.int32),
        jax.ShapeDtypeStruct((SUBLANES, LANES), jnp.int32),
    ]
    out_specs = [rows(D_MODEL), pl.BlockSpec((tm * ROW_TILES, LANES), lambda i: (i, 0)),
                 rows(LANES), rows(LANES), rows(LANES),
                 pl.BlockSpec((SUBLANES, LANES), lambda i: (0, 0))]
    return pl.pallas_call(
        _out_proj_kernel,
        grid=(t // tm,),
        in_specs=[rows(D_MODEL), rows(FOX_WIDTH), rows(GLA_VAL_WIDTH), full(wf), full(wg), full(g2),
                  full(rw_hi), full(rw_lo), full(rb_pad)],
        out_specs=out_specs,
        out_shape=out_shape,
        scratch_shapes=[pltpu.VMEM((1, LANES), F32)],
        compiler_params=pltpu.CompilerParams(
            dimension_semantics=("arbitrary",), vmem_limit_bytes=VMEM_LIMIT),
        name="out_proj_router",
    )(x2, fox, gla, wf, wg, g2, rw_hi, rw_lo, rb_pad)


def _row_copy(src_ref, src_row, dst_ref, dst_row, sem):
    return pltpu.make_async_copy(
        src_ref.at[pl.ds(pl.multiple_of(src_row * ROW_TILES, ROW_TILES), ROW_TILES)],
        dst_ref.at[pl.ds(pl.multiple_of(dst_row * ROW_TILES, ROW_TILES), ROW_TILES)],
        sem)


def _dispatch_kernel(pos_ref, n2_ref, xs_in_ref, xs_ref, sem):
    del xs_in_ref
    tb = pos_ref.shape[0] // TOP_K
    base = pl.program_id(0) * tb

    def issue(t, c):
        for kk in range(TOP_K):
            _row_copy(n2_ref, base + t, xs_ref, pos_ref[t * TOP_K + kk], sem).start()
        return c

    lax.fori_loop(0, tb, issue, 0)

    def drain(t, c):
        for kk in range(TOP_K):
            _row_copy(n2_ref, base + t, xs_ref, pos_ref[t * TOP_K + kk], sem).wait()
        return c

    lax.fori_loop(0, tb, drain, 0)


def _dispatch(pos_flat, n2_tiles, xs_init):
    t = n2_tiles.shape[0] // ROW_TILES
    tb = TB_DISPATCH
    return pl.pallas_call(
        _dispatch_kernel,
        grid=(t // tb,),
        in_specs=[pl.BlockSpec((tb * TOP_K,), lambda i: (i,), memory_space=pltpu.SMEM),
                  pl.BlockSpec(memory_space=pl.ANY),
                  pl.BlockSpec(memory_space=pl.ANY)],
        out_specs=pl.BlockSpec(memory_space=pl.ANY),
        out_shape=jax.ShapeDtypeStruct(xs_init.shape, xs_init.dtype),
        scratch_shapes=[pltpu.SemaphoreType.DMA],
        input_output_aliases={2: 0},
        compiler_params=pltpu.CompilerParams(
            dimension_semantics=("arbitrary",), has_side_effects=True),
        name="dispatch",
    )(pos_flat, n2_tiles, xs_init)


def _experts_kernel(te_ref, nu_ref, xs_ref, wi_ref, bi_ref, wo_ref, bo_ref, ys_ref, wib_ref, wob_ref):
    i = pl.program_id(0)
    tm = xs_ref.shape[0] // ROW_TILES
    prev = te_ref[jnp.maximum(i - 1, 0)]

    @pl.when((i == 0) | (te_ref[i] != prev))
    def _():
        wib_ref[...] = wi_ref[0].astype(BF16)
        wob_ref[...] = wo_ref[0].astype(BF16)

    @pl.when(i < nu_ref[0])
    def _():
        x = jnp.concatenate(
            [xs_ref[pl.ds(c, tm, stride=ROW_TILES), :] for c in range(ROW_TILES)], axis=1)
        h = _dot(x.astype(BF16), wib_ref[...]) + bi_ref[0]
        gate = jnp.minimum(h[:, :D_FF], SWIGLU_LIMIT)
        lin = jnp.clip(h[:, D_FF:], -SWIGLU_LIMIT, SWIGLU_LIMIT)
        a = (lin + 1.0) * (gate * jax.nn.sigmoid(SWIGLU_ALPHA * gate))
        y = _dot(a.astype(BF16), wob_ref[...]) + bo_ref[0]
        for c in range(ROW_TILES):
            ys_ref[pl.ds(c, tm, stride=ROW_TILES), :] = y[:, c * LANES:(c + 1) * LANES]

    @pl.when(i >= nu_ref[0])
    def _():
        ys_ref[...] = jnp.zeros_like(ys_ref)


def _experts(tile_expert, num_used, xs, w_in, b_in, w_out, b_out):
    n_tiles = tile_expert.shape[0]
    tm = TM_EXP
    grid_spec = pltpu.PrefetchScalarGridSpec(
        num_scalar_prefetch=2,
        grid=(n_tiles,),
        in_specs=[
            pl.BlockSpec((tm * ROW_TILES, LANES), lambda i, te, nu: (i, 0)),
            pl.BlockSpec((1, D_MODEL, 2 * D_FF), lambda i, te, nu: (te[i], 0, 0)),
            pl.BlockSpec((1, 1, 2 * D_FF), lambda i, te, nu: (te[i], 0, 0)),
            pl.BlockSpec((1, D_FF, D_MODEL), lambda i, te, nu: (te[i], 0, 0)),
            pl.BlockSpec((1, 1, D_MODEL), lambda i, te, nu: (te[i], 0, 0)),
        ],
        out_specs=pl.BlockSpec((tm * ROW_TILES, LANES), lambda i, te, nu: (i, 0)),
        scratch_shapes=[pltpu.VMEM((D_MODEL, 2 * D_FF), BF16), pltpu.VMEM((D_FF, D_MODEL), BF16)],
    )
    return pl.pallas_call(
        _experts_kernel,
        grid_spec=grid_spec,
        out_shape=jax.ShapeDtypeStruct(xs.shape, F32),
        compiler_params=pltpu.CompilerParams(
            dimension_semantics=("arbitrary",), vmem_limit_bytes=VMEM_LIMIT),
        name="experts",
    )(tile_expert, num_used, xs, w_in, b_in, w_out, b_out)


def _combine_kernel(pos_ref, gate_ref, h_ref, fg_ref, ys_ref, o_ref, gbuf_ref, sem):
    tc = h_ref.shape[0]

    def issue(t, c):
        for kk in range(TOP_K):
            _row_copy(ys_ref, pos_ref[t * TOP_K + kk], gbuf_ref.at[kk], t, sem).start()
        return c

    lax.fori_loop(0, tc, issue, 0)

    def drain(t, c):
        for kk in range(TOP_K):
            _row_copy(ys_ref, pos_ref[t * TOP_K + kk], gbuf_ref.at[kk], t, sem).wait()
        return c

    lax.fori_loop(0, tc, drain, 0)

    gates = gate_ref[...]
    h = h_ref[...]
    for kk in range(TOP_K):
        yk = jnp.concatenate(
            [gbuf_ref[kk, pl.ds(c, tc, stride=ROW_TILES), :] for c in range(ROW_TILES)], axis=1)
        h = h + gates[:, kk:kk + 1] * yk
    o_ref[...] = _rms(h, fg_ref[...])


def _combine(pos_flat, gates, h1, final_g, ys):
    t = h1.shape[0]
    tc = TC_COMBINE
    return pl.pallas_call(
        _combine_kernel,
        grid=(t // tc,),
        in_specs=[pl.BlockSpec((tc * TOP_K,), lambda i: (i,), memory_space=pltpu.SMEM),
                  pl.BlockSpec((tc, LANES), lambda i: (i, 0)),
                  pl.BlockSpec((tc, D_MODEL), lambda i: (i, 0)),
                  pl.BlockSpec((1, D_MODEL), lambda i: (0, 0)),
                  pl.BlockSpec(memory_space=pl.ANY)],
        out_specs=pl.BlockSpec((tc, D_MODEL), lambda i: (i, 0)),
        out_shape=jax.ShapeDtypeStruct((t, D_MODEL), F32),
        scratch_shapes=[pltpu.VMEM((TOP_K, tc * ROW_TILES, LANES), F32), pltpu.SemaphoreType.DMA],
        compiler_params=pltpu.CompilerParams(
            dimension_semantics=("arbitrary",), vmem_limit_bytes=VMEM_LIMIT),
        name="combine",
    )(pos_flat, gates, h1, final_g, ys)


def _hi_lo(w):
    hi = w.astype(BF16)
    return hi, (w - hi.astype(F32)).astype(BF16)


def _layer(x2, batch, seq, norm1_g, w_in, fox_f_bias, gla_gate_up, gla_gate_bias, gla_norm_g, w_out,
           norm2_g, router_w, router_b, exp_w_in, exp_b_in, exp_w_out, exp_b_out, final_g):
    t = batch * seq
    o = 0
    segs = {}
    for name, width in (("fq", FOX_WIDTH), ("fk", FOX_WIDTH), ("fv", FOX_WIDTH), ("ff", FOX_HEADS),
                        ("gq", GLA_KEY_WIDTH), ("gk", GLA_KEY_WIDTH), ("gv", GLA_VAL_WIDTH),
                        ("gl", GLA_GATE_RANK), ("go", GLA_VAL_WIDTH)):
        segs[name] = w_in[:, o:o + width]
        o += width
    w_main = jnp.concatenate([segs[n] for n in ("fq", "fk", "fv", "gq", "gk", "gv", "go")],
                             axis=1).astype(BF16)
    n_small = FOX_HEADS + GLA_GATE_RANK
    w_small = jnp.pad(jnp.concatenate([segs["ff"], segs["gl"]], axis=1), ((0, 0), (0, LANES - n_small)))
    ws_hi, ws_lo = _hi_lo(w_small)
    fb_pad = jnp.pad(fox_f_bias, (0, LANES - FOX_HEADS)).reshape(1, LANES)
    gu_pad = jnp.pad(gla_gate_up, ((FOX_HEADS, LANES - n_small), (0, 0)))
    gu_hi, gu_lo = _hi_lo(gu_pad)

    fq, fk, fv, gq, gk, gv, go, la, f_cum, f_t = _in_proj(
        x2, norm1_g.reshape(1, D_MODEL), w_main, ws_hi, ws_lo, fb_pad, gu_hi, gu_lo,
        gla_gate_bias.reshape(1, GLA_KEY_WIDTH), batch, seq)

    pairs = FOX_HEADS // 2
    f_pairs = f_cum[:, :FOX_HEADS].reshape(batch, seq, pairs, 2).transpose(0, 2, 1, 3)
    ft_pairs = f_t.reshape(batch, pairs, 2, seq)
    fox = _fox(fq, fk, fv, f_pairs, ft_pairs, batch, seq)
    gla = _gla(gq, gk, la, gv, go, gla_norm_g.reshape(1, GLA_VAL_DIM), batch, seq)

    w_out_b = w_out.astype(BF16)
    rw_hi, rw_lo = _hi_lo(jnp.pad(router_w, ((0, 0), (0, LANES - N_EXPERTS))))
    rb_pad = jnp.pad(router_b, (0, LANES - N_EXPERTS), constant_values=NEG_BIG).reshape(1, LANES)
    h1, n2_tiles, ti, gates, rank, counts = _out_proj(
        x2, fox, gla, w_out_b[:FOX_WIDTH], w_out_b[FOX_WIDTH:], norm2_g.reshape(1, D_MODEL),
        rw_hi, rw_lo, rb_pad)

    cnt = counts[0, :N_EXPERTS]
    padded = ((cnt + TM_EXP - 1) // TM_EXP) * TM_EXP
    ends = jnp.cumsum(padded)
    starts = ends - padded
    pos = (starts[ti[:, :TOP_K]] + rank[:, :TOP_K]).reshape(-1).astype(jnp.int32)
    n_tiles = (t * TOP_K) // TM_EXP + N_EXPERTS
    num_used = (ends[-1] // TM_EXP).astype(jnp.int32)
    tile_start = jnp.arange(n_tiles, dtype=jnp.int32) * TM_EXP
    tile_expert = jnp.minimum(jnp.searchsorted(ends, tile_start, side="right"), N_EXPERTS - 1)
    last_expert = tile_expert[jnp.maximum(num_used - 1, 0)]
    tile_expert = jnp.where(jnp.arange(n_tiles) < num_used, tile_expert, last_expert).astype(jnp.int32)

    xs_init = jnp.zeros((n_tiles * TM_EXP * ROW_TILES, LANES), F32)
    xs = _dispatch(pos, n2_tiles, xs_init)
    ys = _experts(tile_expert, num_used.reshape(1), xs, exp_w_in,
                  exp_b_in.reshape(N_EXPERTS, 1, 2 * D_FF), exp_w_out,
                  exp_b_out.reshape(N_EXPERTS, 1, D_MODEL))
    return _combine(pos, gates, h1, final_g.reshape(1, D_MODEL), ys)


def kernel(x, norm1_g, w_in, fox_f_bias, gla_gate_up, gla_gate_bias, gla_norm_g, w_out, norm2_g,
           router_w, router_b, exp_w_in, exp_b_in, exp_w_out, exp_b_out, final_g):
    batch, seq, d = x.shape
    depth = norm1_g.shape[0]
    assert depth == 1 and d == D_MODEL
    out = _layer(x.reshape(batch * seq, d), batch, seq, norm1_g[0], w_in[0], fox_f_bias[0],
                 gla_gate_up[0], gla_gate_bias[0], gla_norm_g[0], w_out[0], norm2_g[0], router_w[0],
                 router_b[0], exp_w_in[0], exp_b_in[0], exp_w_out[0], exp_b_out[0], final_g)
    return out.reshape(batch, seq, d)
```

```python
import functools

import jax
import jax.numpy as jnp
import numpy as np
from jax import lax
from jax.experimental import pallas as pl
from jax.experimental.pallas import tpu as pltpu

F32 = jnp.float32
BF16 = jnp.bfloat16

D_MODEL = 1024
FOX_HEADS = 8
FOX_HEAD_DIM = 64
FOX_WIDTH = 512
GLA_HEADS = 4
GLA_KEY_DIM = 64
GLA_KEY_WIDTH = 256
GLA_VAL_DIM = 128
GLA_VAL_WIDTH = 512
GLA_GATE_RANK = 16
GLA_GATE_TAU = 16.0
CHUNK = 64
N_EXPERTS = 32
TOP_K = 4
D_FF = 1024
SWIGLU_LIMIT = 7.0
SWIGLU_ALPHA = 1.702
EPS = 1e-5

LANES = 128
SUBLANES = 8
ROW_TILES = D_MODEL // LANES
VMEM_LIMIT = 56 * 1024 * 1024

TM_PROJ = 256
TQ = 256
FOX_HEADS_PER_STEP = 4
GLA_ROWS = 512
TM_EXP = 256
TB_DISPATCH = 512
TC_COMBINE = 256
QK_WIDTH = FOX_HEADS * LANES
NEG_BIG = -1e30
LOG2E = 1.4426950408889634


def _log_sigmoid(z):
    return jnp.minimum(z, 0.0) - jnp.log1p(jnp.exp(-jnp.abs(z)))


def _split3(a):
    p1 = a.astype(BF16)
    r1 = a - p1.astype(F32)
    p2 = r1.astype(BF16)
    r2 = r1 - p2.astype(F32)
    return p1, p2, r2.astype(BF16)


def _dot(a, b):
    return jnp.dot(a, b, preferred_element_type=F32)


def _dot_nt(a, b):
    return lax.dot_general(a, b, (((1,), (1,)), ((), ())), preferred_element_type=F32)


def _dot_tn(a, b):
    return lax.dot_general(a, b, (((0,), (0,)), ((), ())), preferred_element_type=F32)


def _dot3_exact_lhs(tri, a):
    p1, p2, p3 = _split3(a)
    return _dot(tri, p1) + _dot(tri, p2) + _dot(tri, p3)


def _dot_hi(a, b_hi, b_lo):
    a_hi = a.astype(BF16)
    a_lo = (a - a_hi.astype(F32)).astype(BF16)
    return _dot(a_hi, b_hi) + _dot(a_lo, b_hi) + _dot(a_hi, b_lo)


def _rms(x, g):
    return x * lax.rsqrt(jnp.mean(x * x, axis=-1, keepdims=True) + EPS) * g


def _in_proj_kernel(x_ref, g_ref, wm_ref, wsh_ref, wsl_ref, fb_ref, guh_ref, gul_ref, gb_ref,
                    selq_ref, selk_ref, oneq_ref, onek_ref,
                    qa_ref, ka_ref, vt_ref, gq_ref, gk_ref, gv_ref, go_ref, la_ref, carry_ref):
    tm = x_ref.shape[0]

    @pl.when(pl.program_id(1) == 0)
    def _():
        carry_ref[...] = jnp.zeros_like(carry_ref)

    n = _rms(x_ref[...], g_ref[...])
    main = _dot(n.astype(BF16), wm_ref[...])

    small = _dot_hi(n, wsh_ref[...], wsl_ref[...])

    ls = _log_sigmoid(small + fb_ref[...])
    row = lax.broadcasted_iota(jnp.int32, (tm, tm), 0)
    col = lax.broadcasted_iota(jnp.int32, (tm, tm), 1)
    tri = jnp.where(row >= col, 1.0, 0.0).astype(BF16)
    cum = _dot3_exact_lhs(tri, ls) + carry_ref[...]
    carry_ref[...] = cum[tm - 1:tm, :]

    pieces = _split3(cum * LOG2E)
    aug_q = oneq_ref[...]
    aug_k = onek_ref[...]
    for i, piece in enumerate(pieces):
        aug_q = aug_q + _dot(piece, selq_ref[i])
        aug_k = aug_k + _dot(piece, selk_ref[i])
    o = 0
    qa_ref[...] = (main[:, o:o + QK_WIDTH] * (LOG2E * FOX_HEAD_DIM ** -0.5) + aug_q).astype(BF16)
    o += QK_WIDTH
    ka_ref[...] = (main[:, o:o + QK_WIDTH] + aug_k).astype(BF16); o += QK_WIDTH
    for p in range(FOX_HEADS // 2):
        vt_ref[p] = main[:, o:o + LANES].T.astype(BF16); o += LANES
    gq_ref[...] = (main[:, o:o + GLA_KEY_WIDTH] * (GLA_KEY_DIM ** -0.5)).astype(BF16); o += GLA_KEY_WIDTH
    gk_ref[...] = main[:, o:o + GLA_KEY_WIDTH]; o += GLA_KEY_WIDTH
    gv_ref[...] = main[:, o:o + GLA_VAL_WIDTH].astype(BF16); o += GLA_VAL_WIDTH
    go_ref[...] = main[:, o:o + GLA_VAL_WIDTH]

    z = _dot_hi(small, guh_ref[...], gul_ref[...]) + gb_ref[...]
    la_ref[...] = _log_sigmoid(z) * (1.0 / GLA_GATE_TAU)


def _aug_constants():
    selq = np.zeros((3, LANES, QK_WIDTH), np.float32)
    selk = np.zeros((3, LANES, QK_WIDTH), np.float32)
    oneq = np.zeros((1, QK_WIDTH), np.float32)
    onek = np.zeros((1, QK_WIDTH), np.float32)
    for h in range(FOX_HEADS):
        base = h * LANES + FOX_HEAD_DIM
        for i in range(3):
            selq[i, h, base + i] = 1.0
            onek[0, base + i] = 1.0
            selk[i, h, base + 3 + i] = -1.0
            oneq[0, base + 3 + i] = 1.0
    return (jnp.asarray(selq, BF16), jnp.asarray(selk, BF16), jnp.asarray(oneq), jnp.asarray(onek))


def _in_proj(x2, norm_g, w_main, ws_hi, ws_lo, fb_pad, gu_hi, gu_lo, gb, batch, seq):
    t = batch * seq
    tm = TM_PROJ
    nj = seq // tm
    pairs = FOX_HEADS // 2
    selq, selk, oneq, onek = _aug_constants()
    rows = lambda w: pl.BlockSpec((tm, w), lambda b, j: (b * nj + j, 0))
    full = lambda a: pl.BlockSpec(a.shape, lambda b, j: (0,) * a.ndim)
    out_shape = [
        jax.ShapeDtypeStruct((t, QK_WIDTH), BF16),
        jax.ShapeDtypeStruct((t, QK_WIDTH), BF16),
        jax.ShapeDtypeStruct((pairs, LANES, t), BF16),
        jax.ShapeDtypeStruct((t, GLA_KEY_WIDTH), BF16),
        jax.ShapeDtypeStruct((t, GLA_KEY_WIDTH), F32),
        jax.ShapeDtypeStruct((t, GLA_VAL_WIDTH), BF16),
        jax.ShapeDtypeStruct((t, GLA_VAL_WIDTH), F32),
        jax.ShapeDtypeStruct((t, GLA_KEY_WIDTH), F32),
    ]
    out_specs = [rows(QK_WIDTH), rows(QK_WIDTH),
                 pl.BlockSpec((pairs, LANES, tm), lambda b, j: (0, 0, b * nj + j)),
                 rows(GLA_KEY_WIDTH), rows(GLA_KEY_WIDTH), rows(GLA_VAL_WIDTH), rows(GLA_VAL_WIDTH),
                 rows(GLA_KEY_WIDTH)]
    args = (x2, norm_g, w_main, ws_hi, ws_lo, fb_pad, gu_hi, gu_lo, gb, selq, selk, oneq, onek)
    return pl.pallas_call(
        _in_proj_kernel,
        grid=(batch, nj),
        in_specs=[rows(D_MODEL)] + [full(a) for a in args[1:]],
        out_specs=out_specs,
        out_shape=out_shape,
        scratch_shapes=[pltpu.VMEM((1, LANES), F32)],
        compiler_params=pltpu.CompilerParams(
            dimension_semantics=("arbitrary", "arbitrary"), vmem_limit_bytes=VMEM_LIMIT),
        name="in_proj",
    )(*args)


def _fox_kernel(q_ref, k_ref, vt_ref, o_ref, acc_ref, m_ref, l_ref, sa_ref, ma_ref, sb_ref, mb_ref):
    tq = q_ref.shape[0]
    tk = tq
    nh = q_ref.shape[1] // LANES
    i = pl.program_id(2)

    m_ref[...] = jnp.full_like(m_ref, -jnp.inf)
    l_ref[...] = jnp.zeros_like(l_ref)
    acc_ref[...] = jnp.zeros_like(acc_ref)

    def scores(start, h):
        return _dot_nt(k_ref[pl.ds(start, tk), h * LANES:(h + 1) * LANES],
                       q_ref[:, h * LANES:(h + 1) * LANES])

    def block_max(s):
        return jnp.max(s.reshape(tk // SUBLANES, SUBLANES, tq), axis=0)

    def accumulate(start, block):
        probs, alphas = [], []
        for h, (s, mb) in enumerate(block):
            m_prev = m_ref[h]
            m_new = jnp.maximum(m_prev, jnp.max(mb, axis=0, keepdims=True))
            p = jnp.exp2(s - m_new[0:1])
            alpha = jnp.exp2(m_prev - m_new)
            l_ref[h] = alpha * l_ref[h] + jnp.sum(p.reshape(tk // SUBLANES, SUBLANES, tq), axis=0)
            m_ref[h] = m_new
            probs.append(p.astype(BF16))
            alphas.append(alpha[0:1])
        for h in range(nh):
            pr, hh = divmod(h, 2)
            vt = vt_ref[pr, hh * FOX_HEAD_DIM:(hh + 1) * FOX_HEAD_DIM, pl.ds(start, tk)]
            acc_ref[h] = alphas[h] * acc_ref[h] + _dot(vt, probs[h])

    diag_start = pl.multiple_of(i * tk, tk)
    key = lax.broadcasted_iota(jnp.int32, (tk, tq), 0)
    qry = lax.broadcasted_iota(jnp.int32, (tk, tq), 1)
    diag = []
    for h in range(nh):
        s = jnp.where(key <= qry, scores(diag_start, h), -jnp.inf)
        diag.append((s, block_max(s)))
    accumulate(diag_start, diag)

    slots = ((sa_ref, ma_ref), (sb_ref, mb_ref))

    def stage_scores(j, slot):
        start = pl.multiple_of(j * tk, tk)
        for h in range(nh):
            s = scores(start, h)
            slots[slot][0][h] = s
            slots[slot][1][h] = block_max(s)

    def stage_accumulate(j, slot):
        start = pl.multiple_of(j * tk, tk)
        accumulate(start, [(slots[slot][0][h], slots[slot][1][h]) for h in range(nh)])

    @pl.when(i > 0)
    def _():
        stage_scores(0, 0)
        pairs = (i - 1) // 2

        def body(jj, c):
            j = 2 * jj
            stage_scores(j + 1, 1)
            stage_accumulate(j, 0)
            stage_scores(j + 2, 0)
            stage_accumulate(j + 1, 1)
            return c

        lax.fori_loop(0, pairs, body, 0)

        @pl.when(i - 1 > 2 * pairs)
        def _():
            stage_scores(i - 1, 1)
            stage_accumulate(i - 2, 0)
            stage_accumulate(i - 1, 1)

        @pl.when(i - 1 == 2 * pairs)
        def _():
            stage_accumulate(i - 1, 0)

    o_t = jnp.concatenate(
        [acc_ref[h] / jnp.sum(l_ref[h], axis=0, keepdims=True) for h in range(nh)], axis=0)
    o_ref[...] = o_t.T.astype(o_ref.dtype)


def _fox(qa, ka, vt, batch, seq):
    t = batch * seq
    nq = seq // TQ
    nh = FOX_HEADS_PER_STEP
    groups = FOX_HEADS // nh
    return pl.pallas_call(
        _fox_kernel,
        grid=(batch, groups, nq),
        in_specs=[
            pl.BlockSpec((TQ, nh * LANES), lambda b, g, i: (b * nq + i, g)),
            pl.BlockSpec((seq, nh * LANES), lambda b, g, i: (b, g)),
            pl.BlockSpec((nh // 2, LANES, seq), lambda b, g, i: (g, 0, b)),
        ],
        out_specs=pl.BlockSpec((TQ, nh * FOX_HEAD_DIM), lambda b, g, i: (b * nq + i, g)),
        out_shape=jax.ShapeDtypeStruct((t, FOX_WIDTH), BF16),
        scratch_shapes=[pltpu.VMEM((nh, FOX_HEAD_DIM, TQ), F32),
                        pltpu.VMEM((nh, SUBLANES, TQ), F32),
                        pltpu.VMEM((nh, SUBLANES, TQ), F32),
                        pltpu.VMEM((nh, TQ, TQ), F32),
                        pltpu.VMEM((nh, SUBLANES, TQ), F32),
                        pltpu.VMEM((nh, TQ, TQ), F32),
                        pltpu.VMEM((nh, SUBLANES, TQ), F32)],
        compiler_params=pltpu.CompilerParams(
            dimension_semantics=("parallel", "parallel", "arbitrary"), vmem_limit_bytes=VMEM_LIMIT),
        name="fox_attention",
    )(qa, ka, vt)


def _gla_kernel(q_ref, k_ref, la_ref, v_ref, go_ref, ng_ref, o_ref, st_ref, obuf_ref):
    rows = q_ref.shape[0]

    @pl.when(pl.program_id(2) == 0)
    def _():
        st_ref[...] = jnp.zeros_like(st_ref)

    r = lax.broadcasted_iota(jnp.int32, (CHUNK, CHUNK), 0)
    c = lax.broadcasted_iota(jnp.int32, (CHUNK, CHUNK), 1)
    tri = jnp.where(r >= c, 1.0, 0.0).astype(BF16)
    vrow = lax.broadcasted_iota(jnp.int32, (2 * GLA_VAL_DIM, 2 * GLA_KEY_DIM), 0)
    kcol = lax.broadcasted_iota(jnp.int32, (2 * GLA_VAL_DIM, 2 * GLA_KEY_DIM), 1)
    same_head = (vrow >= GLA_VAL_DIM) == (kcol >= GLA_KEY_DIM)

    for ch in range(rows // CHUNK):
        sl = slice(ch * CHUNK, (ch + 1) * CHUNK)
        cum = _dot3_exact_lhs(tri, la_ref[sl, :])
        tot = cum[CHUNK - 1:CHUNK, :]
        kd = (k_ref[sl, :] * jnp.exp(tot - cum)).astype(BF16)
        upd_t = _dot_tn(v_ref[sl, :], kd)
        st = st_ref[...] * jnp.exp(tot) + jnp.where(same_head, upd_t, 0.0)
        st_ref[...] = st
        obuf_ref[sl, :] = _dot_nt(q_ref[sl, :], st.astype(BF16))

    o = obuf_ref[...]
    go = go_ref[...]
    ng = ng_ref[...]
    halves = []
    for h in range(2):
        hs = slice(h * GLA_VAL_DIM, (h + 1) * GLA_VAL_DIM)
        oh = o[:, hs]
        oh = oh * lax.rsqrt(jnp.mean(oh * oh, axis=-1, keepdims=True) + EPS) * ng
        gh = go[:, hs]
        halves.append(oh * (gh * jax.nn.sigmoid(gh)))
    o_ref[...] = jnp.concatenate(halves, axis=1).astype(o_ref.dtype)


def _gla(gq, gk, la, gv, go, ng, batch, seq):
    t = batch * seq
    nr = seq // GLA_ROWS
    pairs = GLA_HEADS // 2
    kspec = pl.BlockSpec((GLA_ROWS, 2 * GLA_KEY_DIM), lambda b, p, i: (b * nr + i, p))
    vspec = pl.BlockSpec((GLA_ROWS, 2 * GLA_VAL_DIM), lambda b, p, i: (b * nr + i, p))
    return pl.pallas_call(
        _gla_kernel,
        grid=(batch, pairs, nr),
        in_specs=[kspec, kspec, kspec, vspec, vspec,
                  pl.BlockSpec((1, GLA_VAL_DIM), lambda b, p, i: (0, 0))],
        out_specs=vspec,
        out_shape=jax.ShapeDtypeStruct((t, GLA_VAL_WIDTH), BF16),
        scratch_shapes=[pltpu.VMEM((2 * GLA_VAL_DIM, 2 * GLA_KEY_DIM), F32),
                        pltpu.VMEM((GLA_ROWS, 2 * GLA_VAL_DIM), F32)],
        compiler_params=pltpu.CompilerParams(
            dimension_semantics=("parallel", "parallel", "arbitrary"), vmem_limit_bytes=VMEM_LIMIT),
        name="gla",
    )(gq, gk, la, gv, go, ng)


def _out_proj_kernel(x_ref, fox_ref, gla_ref, wf_ref, wg_ref, g2_ref, rwh_ref, rwl_ref, rb_ref,
                     h_ref, n2_ref, ti_ref, gate_ref, rank_ref, cnt_ref, carry_ref):
    tm = x_ref.shape[0]

    @pl.when(pl.program_id(0) == 0)
    def _():
        carry_ref[...] = jnp.zeros_like(carry_ref)

    h = x_ref[...] + _dot(fox_ref[...], wf_ref[...]) + _dot(gla_ref[...], wg_ref[...])
    h_ref[...] = h
    n2 = _rms(h, g2_ref[...])
    for c in range(ROW_TILES):
        n2_ref[pl.ds(c, tm, stride=ROW_TILES), :] = n2[:, c * LANES:(c + 1) * LANES]

    logits = _dot_hi(n2, rwh_ref[...], rwl_ref[...]) + rb_ref[...]
    lane = lax.broadcasted_iota(jnp.int32, (tm, LANES), 1)
    work = logits
    vals, idxs = [], []
    for _ in range(TOP_K):
        m = jnp.max(work, axis=-1, keepdims=True)
        idx = jnp.min(jnp.where(work == m, lane, LANES), axis=-1, keepdims=True)
        vals.append(m)
        idxs.append(idx)
        work = jnp.where(lane == idx, -jnp.inf, work)
    exps = [jnp.exp(v - vals[0]) for v in vals]
    denom = exps[0] + exps[1] + exps[2] + exps[3]

    onehots = [lane == idx for idx in idxs]
    chosen = jnp.zeros((tm, LANES), F32)
    for oh in onehots:
        chosen = chosen + jnp.where(oh, 1.0, 0.0)
    row = lax.broadcasted_iota(jnp.int32, (tm, tm), 0)
    col = lax.broadcasted_iota(jnp.int32, (tm, tm), 1)
    strict = jnp.where(row > col, 1.0, 0.0).astype(BF16)
    before = _dot(strict, chosen.astype(BF16)) + carry_ref[...]
    carry = carry_ref[...] + jnp.sum(chosen, axis=0, keepdims=True)
    carry_ref[...] = carry
    cnt_ref[...] = jnp.broadcast_to(carry, cnt_ref.shape).astype(jnp.int32)

    ti = jnp.zeros((tm, LANES), jnp.int32)
    gates = jnp.zeros((tm, LANES), F32)
    ranks = jnp.zeros((tm, LANES), F32)
    for kk in range(TOP_K):
        sel = lane == kk
        ti = jnp.where(sel, idxs[kk], ti)
        gates = jnp.where(sel, exps[kk] / denom, gates)
        rk = jnp.sum(jnp.where(onehots[kk], before, 0.0), axis=-1, keepdims=True)
        ranks = jnp.where(sel, rk, ranks)
    ti_ref[...] = ti
    gate_ref[...] = gates
    rank_ref[...] = ranks.astype(jnp.int32)


def _out_proj(x2, fox, gla, wf, wg, g2, rw_hi, rw_lo, rb_pad):
    t = x2.shape[0]
    tm = TM_PROJ
    rows = lambda w: pl.BlockSpec((tm, w), lambda i: (i, 0))
    full = lambda a: pl.BlockSpec(a.shape, lambda i: (0,) * a.ndim)
    out_shape = [
        jax.ShapeDtypeStruct((t, D_MODEL), F32),
        jax.ShapeDtypeStruct((t * ROW_TILES, LANES), F32),
        jax.ShapeDtypeStruct((t, LANES), jnp.int32),
        jax.ShapeDtypeStruct((t, LANES), F32),
        jax.ShapeDtypeStruct((t, LANES), jnp.int32),
        jax.ShapeDtypeStruct((SUBLANES, LANES), jnp.int32),
    ]
    out_specs = [rows(D_MODEL), pl.BlockSpec((tm * ROW_TILES, LANES), lambda i: (i, 0)),
                 rows(LANES), rows(LANES), rows(LANES),
                 pl.BlockSpec((SUBLANES, LANES), lambda i: (0, 0))]
    return pl.pallas_call(
        _out_proj_kernel,
        grid=(t // tm,),
        in_specs=[rows(D_MODEL), rows(FOX_WIDTH), rows(GLA_VAL_WIDTH), full(wf), full(wg), full(g2),
                  full(rw_hi), full(rw_lo), full(rb_pad)],
        out_specs=out_specs,
        out_shape=out_shape,
        scratch_shapes=[pltpu.VMEM((1, LANES), F32)],
        compiler_params=pltpu.CompilerParams(
            dimension_semantics=("arbitrary",), vmem_limit_bytes=VMEM_LIMIT),
        name="out_proj_router",
    )(x2, fox, gla, wf, wg, g2, rw_hi, rw_lo, rb_pad)


def _row_copy(src_ref, src_row, dst_ref, dst_row, sem):
    return pltpu.make_async_copy(
        src_ref.at[pl.ds(pl.multiple_of(src_row * ROW_TILES, ROW_TILES), ROW_TILES)],
        dst_ref.at[pl.ds(pl.multiple_of(dst_row * ROW_TILES, ROW_TILES), ROW_TILES)],
        sem)


def _dispatch_kernel(pos_ref, n2_ref, xs_in_ref, xs_ref, sem):
    del xs_in_ref
    tb = pos_ref.shape[0] // TOP_K

    def issue(t, c):
        for kk in range(TOP_K):
            _row_copy(n2_ref, t, xs_ref, pos_ref[t * TOP_K + kk], sem).start()
        return c

    lax.fori_loop(0, tb, issue, 0)
    for kk in range(TOP_K):
        pltpu.make_async_copy(n2_ref, xs_ref.at[pl.ds(0, tb * ROW_TILES)], sem).wait()


def _dispatch(pos_flat, n2_tiles, xs_init):
    t = n2_tiles.shape[0] // ROW_TILES
    tb = TB_DISPATCH
    return pl.pallas_call(
        _dispatch_kernel,
        grid=(t // tb,),
        in_specs=[pl.BlockSpec((tb * TOP_K,), lambda i: (i,), memory_space=pltpu.SMEM),
                  pl.BlockSpec((tb * ROW_TILES, LANES), lambda i: (i, 0)),
                  pl.BlockSpec(memory_space=pl.ANY)],
        out_specs=pl.BlockSpec(memory_space=pl.ANY),
        out_shape=jax.ShapeDtypeStruct(xs_init.shape, xs_init.dtype),
        scratch_shapes=[pltpu.SemaphoreType.DMA],
        input_output_aliases={2: 0},
        compiler_params=pltpu.CompilerParams(
            dimension_semantics=("arbitrary",), vmem_limit_bytes=VMEM_LIMIT),
        name="dispatch",
    )(pos_flat, n2_tiles, xs_init)


def _experts_kernel(te_ref, nu_ref, xs_ref, wi_ref, bi_ref, wo_ref, bo_ref, ys_ref, wib_ref, wob_ref):
    i = pl.program_id(0)
    tm = xs_ref.shape[0] // ROW_TILES
    prev = te_ref[jnp.maximum(i - 1, 0)]

    @pl.when((i == 0) | (te_ref[i] != prev))
    def _():
        wib_ref[...] = wi_ref[0].astype(BF16)
        wob_ref[...] = wo_ref[0].astype(BF16)

    @pl.when(i < nu_ref[0])
    def _():
        x = jnp.concatenate(
            [xs_ref[pl.ds(c, tm, stride=ROW_TILES), :] for c in range(ROW_TILES)], axis=1)
        h = _dot(x.astype(BF16), wib_ref[...]) + bi_ref[0]
        gate = jnp.minimum(h[:, :D_FF], SWIGLU_LIMIT)
        lin = jnp.clip(h[:, D_FF:], -SWIGLU_LIMIT, SWIGLU_LIMIT)
        a = (lin + 1.0) * (gate * jax.nn.sigmoid(SWIGLU_ALPHA * gate))
        y = _dot(a.astype(BF16), wob_ref[...]) + bo_ref[0]
        for c in range(ROW_TILES):
            ys_ref[pl.ds(c, tm, stride=ROW_TILES), :] = y[:, c * LANES:(c + 1) * LANES]

    @pl.when(i >= nu_ref[0])
    def _():
        ys_ref[...] = jnp.zeros_like(ys_ref)


def _experts(tile_expert, num_used, xs, w_in, b_in, w_out, b_out):
    n_tiles = tile_expert.shape[0]
    tm = TM_EXP
    grid_spec = pltpu.PrefetchScalarGridSpec(
        num_scalar_prefetch=2,
        grid=(n_tiles,),
        in_specs=[
            pl.BlockSpec((tm * ROW_TILES, LANES), lambda i, te, nu: (i, 0)),
            pl.BlockSpec((1, D_MODEL, 2 * D_FF), lambda i, te, nu: (te[i], 0, 0)),
            pl.BlockSpec((1, 1, 2 * D_FF), lambda i, te, nu: (te[i], 0, 0)),
            pl.BlockSpec((1, D_FF, D_MODEL), lambda i, te, nu: (te[i], 0, 0)),
            pl.BlockSpec((1, 1, D_MODEL), lambda i, te, nu: (te[i], 0, 0)),
        ],
        out_specs=pl.BlockSpec((tm * ROW_TILES, LANES), lambda i, te, nu: (i, 0)),
        scratch_shapes=[pltpu.VMEM((D_MODEL, 2 * D_FF), BF16), pltpu.VMEM((D_FF, D_MODEL), BF16)],
    )
    return pl.pallas_call(
        _experts_kernel,
        grid_spec=grid_spec,
        out_shape=jax.ShapeDtypeStruct(xs.shape, F32),
        compiler_params=pltpu.CompilerParams(
            dimension_semantics=("arbitrary",), vmem_limit_bytes=VMEM_LIMIT),
        name="experts",
    )(tile_expert, num_used, xs, w_in, b_in, w_out, b_out)


def _combine_kernel(pos_ref, gate_ref, h_ref, fg_ref, ys_ref, o_ref, gbuf_ref, sem):
    tc = h_ref.shape[0]

    def issue(t, c):
        for kk in range(TOP_K):
            _row_copy(ys_ref, pos_ref[t * TOP_K + kk], gbuf_ref.at[kk], t, sem).start()
        return c

    lax.fori_loop(0, tc, issue, 0)
    for kk in range(TOP_K):
        pltpu.make_async_copy(ys_ref.at[pl.ds(0, tc * ROW_TILES)], gbuf_ref.at[kk], sem).wait()

    gates = gate_ref[...]
    h = h_ref[...]
    for kk in range(TOP_K):
        yk = jnp.concatenate(
            [gbuf_ref[kk, pl.ds(c, tc, stride=ROW_TILES), :] for c in range(ROW_TILES)], axis=1)
        h = h + gates[:, kk:kk + 1] * yk
    o_ref[...] = _rms(h, fg_ref[...])


def _combine(pos_flat, gates, h1, final_g, ys):
    t = h1.shape[0]
    tc = TC_COMBINE
    return pl.pallas_call(
        _combine_kernel,
        grid=(t // tc,),
        in_specs=[pl.BlockSpec((tc * TOP_K,), lambda i: (i,), memory_space=pltpu.SMEM),
                  pl.BlockSpec((tc, LANES), lambda i: (i, 0)),
                  pl.BlockSpec((tc, D_MODEL), lambda i: (i, 0)),
                  pl.BlockSpec((1, D_MODEL), lambda i: (0, 0)),
                  pl.BlockSpec(memory_space=pl.ANY)],
        out_specs=pl.BlockSpec((tc, D_MODEL), lambda i: (i, 0)),
        out_shape=jax.ShapeDtypeStruct((t, D_MODEL), F32),
        scratch_shapes=[pltpu.VMEM((TOP_K, tc * ROW_TILES, LANES), F32), pltpu.SemaphoreType.DMA],
        compiler_params=pltpu.CompilerParams(
            dimension_semantics=("arbitrary",), vmem_limit_bytes=VMEM_LIMIT),
        name="combine",
    )(pos_flat, gates, h1, final_g, ys)


def _hi_lo(w):
    hi = w.astype(BF16)
    return hi, (w - hi.astype(F32)).astype(BF16)


def _layer(x2, batch, seq, norm1_g, w_in, fox_f_bias, gla_gate_up, gla_gate_bias, gla_norm_g, w_out,
           norm2_g, router_w, router_b, exp_w_in, exp_b_in, exp_w_out, exp_b_out, final_g):
    t = batch * seq
    o = 0
    segs = {}
    for name, width in (("fq", FOX_WIDTH), ("fk", FOX_WIDTH), ("fv", FOX_WIDTH), ("ff", FOX_HEADS),
                        ("gq", GLA_KEY_WIDTH), ("gk", GLA_KEY_WIDTH), ("gv", GLA_VAL_WIDTH),
                        ("gl", GLA_GATE_RANK), ("go", GLA_VAL_WIDTH)):
        segs[name] = w_in[:, o:o + width]
        o += width
    def head_groups(w):
        w = w.reshape(D_MODEL, FOX_HEADS, FOX_HEAD_DIM)
        return jnp.pad(w, ((0, 0), (0, 0), (0, LANES - FOX_HEAD_DIM))).reshape(D_MODEL, QK_WIDTH)

    w_main = jnp.concatenate(
        [head_groups(segs["fq"]), head_groups(segs["fk"])]
        + [segs[n] for n in ("fv", "gq", "gk", "gv", "go")], axis=1).astype(BF16)
    n_small = FOX_HEADS + GLA_GATE_RANK
    w_small = jnp.pad(jnp.concatenate([segs["ff"], segs["gl"]], axis=1), ((0, 0), (0, LANES - n_small)))
    ws_hi, ws_lo = _hi_lo(w_small)
    fb_pad = jnp.pad(fox_f_bias, (0, LANES - FOX_HEADS)).reshape(1, LANES)
    gu_pad = jnp.pad(gla_gate_up, ((FOX_HEADS, LANES - n_small), (0, 0)))
    gu_hi, gu_lo = _hi_lo(gu_pad)

    qa, ka, vt, gq, gk, gv, go, la = _in_proj(
        x2, norm1_g.reshape(1, D_MODEL), w_main, ws_hi, ws_lo, fb_pad, gu_hi, gu_lo,
        gla_gate_bias.reshape(1, GLA_KEY_WIDTH), batch, seq)

    fox = _fox(qa, ka, vt, batch, seq)
    gla = _gla(gq, gk, la, gv, go, gla_norm_g.reshape(1, GLA_VAL_DIM), batch, seq)

    w_out_b = w_out.astype(BF16)
    rw_hi, rw_lo = _hi_lo(jnp.pad(router_w, ((0, 0), (0, LANES - N_EXPERTS))))
    rb_pad = jnp.pad(router_b, (0, LANES - N_EXPERTS), constant_values=NEG_BIG).reshape(1, LANES)
    h1, n2_tiles, ti, gates, rank, counts = _out_proj(
        x2, fox, gla, w_out_b[:FOX_WIDTH], w_out_b[FOX_WIDTH:], norm2_g.reshape(1, D_MODEL),
        rw_hi, rw_lo, rb_pad)

    cnt = counts[0, :N_EXPERTS]
    padded = ((cnt + TM_EXP - 1) // TM_EXP) * TM_EXP
    ends = jnp.cumsum(padded)
    starts = ends - padded
    pos = (starts[ti[:, :TOP_K]] + rank[:, :TOP_K]).reshape(-1).astype(jnp.int32)
    n_tiles = (t * TOP_K) // TM_EXP + N_EXPERTS
    num_used = (ends[-1] // TM_EXP).astype(jnp.int32)
    tile_start = jnp.arange(n_tiles, dtype=jnp.int32) * TM_EXP
    tile_expert = jnp.minimum(
        jnp.sum((tile_start[:, None] >= ends[None, :]).astype(jnp.int32), axis=1), N_EXPERTS - 1)
    last_expert = tile_expert[jnp.maximum(num_used - 1, 0)]
    tile_expert = jnp.where(jnp.arange(n_tiles) < num_used, tile_expert, last_expert).astype(jnp.int32)

    xs_init = jnp.zeros((n_tiles * TM_EXP * ROW_TILES, LANES), F32)
    xs = _dispatch(pos, n2_tiles, xs_init)
    ys = _experts(tile_expert, num_used.reshape(1), xs, exp_w_in,
                  exp_b_in.reshape(N_EXPERTS, 1, 2 * D_FF), exp_w_out,
                  exp_b_out.reshape(N_EXPERTS, 1, D_MODEL))
    return _combine(pos, gates, h1, final_g.reshape(1, D_MODEL), ys)


def kernel(x, norm1_g, w_in, fox_f_bias, gla_gate_up, gla_gate_bias, gla_norm_g, w_out, norm2_g,
           router_w, router_b, exp_w_in, exp_b_in, exp_w_out, exp_b_out, final_g):
    batch, seq, d = x.shape
    depth = norm1_g.shape[0]
    assert depth == 1 and d == D_MODEL
    out = _layer(x.reshape(batch * seq, d), batch, seq, norm1_g[0], w_in[0], fox_f_bias[0],
                 gla_gate_up[0], gla_gate_bias[0], gla_norm_g[0], w_out[0], norm2_g[0], router_w[0],
                 router_b[0], exp_w_in[0], exp_b_in[0], exp_w_out[0], exp_b_out[0], final_g)
    return out.reshape(batch, seq, d)
```

```python
import functools

import jax
import jax.numpy as jnp
import numpy as np
from jax import lax
from jax.experimental import pallas as pl
from jax.experimental.pallas import tpu as pltpu

F32 = jnp.float32
BF16 = jnp.bfloat16

D_MODEL = 1024
FOX_HEADS = 8
FOX_HEAD_DIM = 64
FOX_WIDTH = 512
GLA_HEADS = 4
GLA_KEY_DIM = 64
GLA_KEY_WIDTH = 256
GLA_VAL_DIM = 128
GLA_VAL_WIDTH = 512
GLA_GATE_RANK = 16
GLA_GATE_TAU = 16.0
CHUNK = 64
N_EXPERTS = 32
TOP_K = 4
D_FF = 1024
SWIGLU_LIMIT = 7.0
SWIGLU_ALPHA = 1.702
EPS = 1e-5

LANES = 128
SUBLANES = 8
ROW_TILES = D_MODEL // LANES
VMEM_LIMIT = 56 * 1024 * 1024

TM_PROJ = 256
TQ = 256
FOX_HEADS_PER_STEP = 4
GLA_ROWS = 512
TM_EXP = 256
TB_DISPATCH = 512
TC_COMBINE = 256
QK_WIDTH = FOX_HEADS * LANES
F_COPIES = (0, FOX_HEADS + GLA_GATE_RANK, 2 * FOX_HEADS + GLA_GATE_RANK)
F_COPY2, F_COPY3 = F_COPIES[1:]
VT_ROWS = FOX_HEAD_DIM + 16
NEG_BIG = -1e30
LOG2E = 1.4426950408889634


def _log_sigmoid(z):
    return jnp.minimum(z, 0.0) - jnp.log1p(jnp.exp(-jnp.abs(z)))


def _split3(a):
    p1 = a.astype(BF16)
    r1 = a - p1.astype(F32)
    p2 = r1.astype(BF16)
    r2 = r1 - p2.astype(F32)
    return p1, p2, r2.astype(BF16)


def _dot(a, b):
    return jnp.dot(a, b, preferred_element_type=F32)


def _dot_nt(a, b):
    return lax.dot_general(a, b, (((1,), (1,)), ((), ())), preferred_element_type=F32)


def _dot_tn(a, b):
    return lax.dot_general(a, b, (((0,), (0,)), ((), ())), preferred_element_type=F32)


def _dot3_exact_lhs(tri, a):
    p1, p2, p3 = _split3(a)
    return _dot(tri, p1) + _dot(tri, p2) + _dot(tri, p3)


def _dot_hi(a, b_hi, b_lo):
    a_hi = a.astype(BF16)
    a_lo = (a - a_hi.astype(F32)).astype(BF16)
    return _dot(a_hi, b_hi) + _dot(a_lo, b_hi) + _dot(a_hi, b_lo)


def _rms(x, g):
    return x * lax.rsqrt(jnp.mean(x * x, axis=-1, keepdims=True) + EPS) * g


def _in_proj_kernel(x_ref, g_ref, wm_ref, wsh_ref, wsl_ref, fb_ref, guh_ref, gul_ref, gb_ref,
                    selq_ref, selk_ref, oneq_ref, onek_ref,
                    qa_ref, ka_ref, vt_ref, gq_ref, gk_ref, gv_ref, go_ref, la_ref, carry_ref):
    tm = x_ref.shape[0]

    @pl.when(pl.program_id(1) == 0)
    def _():
        carry_ref[...] = jnp.zeros_like(carry_ref)

    n = _rms(x_ref[...], g_ref[...])
    main = _dot(n.astype(BF16), wm_ref[...])

    small = _dot_hi(n, wsh_ref[...], wsl_ref[...])

    ls = _log_sigmoid(small + fb_ref[...])
    row = lax.broadcasted_iota(jnp.int32, (tm, tm), 0)
    col = lax.broadcasted_iota(jnp.int32, (tm, tm), 1)
    tri = jnp.where(row >= col, 1.0, 0.0).astype(BF16)
    cum = _dot3_exact_lhs(tri, ls) + carry_ref[...]
    carry_ref[...] = cum[tm - 1:tm, :]

    lane = lax.broadcasted_iota(jnp.int32, (tm, LANES), 1)
    p1, p2, p3 = _split3(cum * LOG2E)
    zero = jnp.zeros_like(p1)
    comb = jnp.where(lane < FOX_HEADS, p1,
                     jnp.where((lane >= F_COPY2) & (lane < F_COPY2 + FOX_HEADS), p2,
                               jnp.where((lane >= F_COPY3) & (lane < F_COPY3 + FOX_HEADS), p3, zero)))
    aug_q = _dot(comb, selq_ref[...]) + oneq_ref[...]
    aug_k = _dot(comb, selk_ref[...]) + onek_ref[...]
    lane1 = lax.broadcasted_iota(jnp.int32, (1, LANES), 1)
    q_scale = jnp.where(lane1 < FOX_HEAD_DIM, LOG2E * FOX_HEAD_DIM ** -0.5, 0.0)
    k_scale = jnp.where(lane1 < FOX_HEAD_DIM, 1.0, 0.0)
    ones_row = jnp.where(lax.broadcasted_iota(jnp.int32, (VT_ROWS - FOX_HEAD_DIM, tm), 0) == 0, 1.0, 0.0)
    for p in range(FOX_HEADS // 2):
        qc = main[:, p * LANES:(p + 1) * LANES]
        kc = main[:, FOX_WIDTH + p * LANES:FOX_WIDTH + (p + 1) * LANES]
        vc_t = main[:, 2 * FOX_WIDTH + p * LANES:2 * FOX_WIDTH + (p + 1) * LANES].T
        for hh in range(2):
            h = 2 * p + hh
            hs = slice(h * LANES, (h + 1) * LANES)
            if hh:
                qc = pltpu.roll(qc, FOX_HEAD_DIM, 1)
                kc = pltpu.roll(kc, FOX_HEAD_DIM, 1)
            qa_ref[:, hs] = (qc * q_scale + aug_q[:, hs]).astype(BF16)
            ka_ref[:, hs] = (kc * k_scale + aug_k[:, hs]).astype(BF16)
            vt_ref[h] = jnp.concatenate(
                [vc_t[hh * FOX_HEAD_DIM:(hh + 1) * FOX_HEAD_DIM], ones_row], axis=0).astype(BF16)
    o = 3 * FOX_WIDTH
    gq_ref[...] = (main[:, o:o + GLA_KEY_WIDTH] * (GLA_KEY_DIM ** -0.5)).astype(BF16); o += GLA_KEY_WIDTH
    gk_ref[...] = main[:, o:o + GLA_KEY_WIDTH]; o += GLA_KEY_WIDTH
    gv_ref[...] = main[:, o:o + GLA_VAL_WIDTH].astype(BF16); o += GLA_VAL_WIDTH
    go_ref[...] = main[:, o:o + GLA_VAL_WIDTH]

    z = _dot_hi(small, guh_ref[...], gul_ref[...]) + gb_ref[...]
    la_ref[...] = _log_sigmoid(z) * (1.0 / GLA_GATE_TAU)


def _aug_constants():
    selq = np.zeros((LANES, QK_WIDTH), np.float32)
    selk = np.zeros((LANES, QK_WIDTH), np.float32)
    oneq = np.zeros((1, QK_WIDTH), np.float32)
    onek = np.zeros((1, QK_WIDTH), np.float32)
    for h in range(FOX_HEADS):
        base = h * LANES + FOX_HEAD_DIM
        for i, copy in enumerate(F_COPIES):
            selq[copy + h, base + i] = 1.0
            onek[0, base + i] = 1.0
            selk[copy + h, base + 3 + i] = -1.0
            oneq[0, base + 3 + i] = 1.0
    return (jnp.asarray(selq, BF16), jnp.asarray(selk, BF16), jnp.asarray(oneq), jnp.asarray(onek))


def _in_proj(x2, norm_g, w_main, ws_hi, ws_lo, fb_pad, gu_hi, gu_lo, gb, batch, seq):
    t = batch * seq
    tm = TM_PROJ
    nj = seq // tm
    pairs = FOX_HEADS // 2
    selq, selk, oneq, onek = _aug_constants()
    rows = lambda w: pl.BlockSpec((tm, w), lambda b, j: (b * nj + j, 0))
    full = lambda a: pl.BlockSpec(a.shape, lambda b, j: (0,) * a.ndim)
    out_shape = [
        jax.ShapeDtypeStruct((t, QK_WIDTH), BF16),
        jax.ShapeDtypeStruct((t, QK_WIDTH), BF16),
        jax.ShapeDtypeStruct((FOX_HEADS, VT_ROWS, t), BF16),
        jax.ShapeDtypeStruct((t, GLA_KEY_WIDTH), BF16),
        jax.ShapeDtypeStruct((t, GLA_KEY_WIDTH), F32),
        jax.ShapeDtypeStruct((t, GLA_VAL_WIDTH), BF16),
        jax.ShapeDtypeStruct((t, GLA_VAL_WIDTH), F32),
        jax.ShapeDtypeStruct((t, GLA_KEY_WIDTH), F32),
    ]
    out_specs = [rows(QK_WIDTH), rows(QK_WIDTH),
                 pl.BlockSpec((FOX_HEADS, VT_ROWS, tm), lambda b, j: (0, 0, b * nj + j)),
                 rows(GLA_KEY_WIDTH), rows(GLA_KEY_WIDTH), rows(GLA_VAL_WIDTH), rows(GLA_VAL_WIDTH),
                 rows(GLA_KEY_WIDTH)]
    args = (x2, norm_g, w_main, ws_hi, ws_lo, fb_pad, gu_hi, gu_lo, gb, selq, selk, oneq, onek)
    return pl.pallas_call(
        _in_proj_kernel,
        grid=(batch, nj),
        in_specs=[rows(D_MODEL)] + [full(a) for a in args[1:]],
        out_specs=out_specs,
        out_shape=out_shape,
        scratch_shapes=[pltpu.VMEM((1, LANES), F32)],
        compiler_params=pltpu.CompilerParams(
            dimension_semantics=("arbitrary", "arbitrary"), vmem_limit_bytes=VMEM_LIMIT),
        name="in_proj",
    )(*args)


def _fox_kernel(q_ref, k_ref, vt_ref, o_ref, acc_ref, m_ref, sa_ref, ma_ref, sb_ref, mb_ref):
    tq = q_ref.shape[0]
    tk = tq
    nh = q_ref.shape[1] // LANES
    i = pl.program_id(2)

    m_ref[...] = jnp.full_like(m_ref, -jnp.inf)
    acc_ref[...] = jnp.zeros_like(acc_ref)
    slots = ((sa_ref, ma_ref), (sb_ref, mb_ref))

    def stage_scores(block, slot, masked=False):
        start = pl.multiple_of(block * tk, tk)
        for h in range(nh):
            s = _dot_nt(k_ref[pl.ds(start, tk), h * LANES:(h + 1) * LANES],
                        q_ref[:, h * LANES:(h + 1) * LANES])
            if masked:
                key = lax.broadcasted_iota(jnp.int32, (tk, tq), 0)
                qry = lax.broadcasted_iota(jnp.int32, (tk, tq), 1)
                s = jnp.where(key <= qry, s, -jnp.inf)
            slots[slot][0][h] = s
            slots[slot][1][h] = jnp.max(s.reshape(tk // SUBLANES, SUBLANES, tq), axis=0)

    def stage_accumulate(block, slot):
        start = pl.multiple_of(block * tk, tk)
        probs, alphas = [], []
        for h in range(nh):
            m_prev = m_ref[h]
            m_new = jnp.maximum(m_prev, jnp.max(slots[slot][1][h], axis=0, keepdims=True))
            probs.append(jnp.exp2(slots[slot][0][h] - m_new[0:1]).astype(BF16))
            alphas.append(jnp.exp2(m_prev - m_new)[0:1])
            m_ref[h] = m_new
        for h in range(nh):
            acc_ref[h] = alphas[h] * acc_ref[h] + _dot(vt_ref[h, :, pl.ds(start, tk)], probs[h])

    def processed(m):
        return jnp.where(m == 0, i, m - 1)

    stage_scores(i, 0, masked=True)
    pairs = i // 2

    def body(mm, c):
        m = 2 * mm
        stage_scores(m, 1)
        stage_accumulate(processed(m), 0)
        stage_scores(m + 1, 0)
        stage_accumulate(m, 1)
        return c

    lax.fori_loop(0, pairs, body, 0)
    last = 2 * pairs

    @pl.when(i > last)
    def _():
        stage_scores(last, 1)
        stage_accumulate(processed(last), 0)
        stage_accumulate(last, 1)

    @pl.when(i == last)
    def _():
        stage_accumulate(processed(last), 0)

    o_t = jnp.concatenate(
        [acc_ref[h, 0:FOX_HEAD_DIM, :] / acc_ref[h, FOX_HEAD_DIM:FOX_HEAD_DIM + 1, :]
         for h in range(nh)], axis=0)
    o_ref[...] = o_t.T.astype(o_ref.dtype)


def _fox(qa, ka, vt, batch, seq):
    t = batch * seq
    nq = seq // TQ
    nh = FOX_HEADS_PER_STEP
    groups = FOX_HEADS // nh
    return pl.pallas_call(
        _fox_kernel,
        grid=(batch, groups, nq),
        in_specs=[
            pl.BlockSpec((TQ, nh * LANES), lambda b, g, i: (b * nq + i, g)),
            pl.BlockSpec((seq, nh * LANES), lambda b, g, i: (b, g)),
            pl.BlockSpec((nh, VT_ROWS, seq), lambda b, g, i: (g, 0, b)),
        ],
        out_specs=pl.BlockSpec((TQ, nh * FOX_HEAD_DIM), lambda b, g, i: (b * nq + i, g)),
        out_shape=jax.ShapeDtypeStruct((t, FOX_WIDTH), BF16),
        scratch_shapes=[pltpu.VMEM((nh, VT_ROWS, TQ), F32),
                        pltpu.VMEM((nh, SUBLANES, TQ), F32),
                        pltpu.VMEM((nh, TQ, TQ), F32),
                        pltpu.VMEM((nh, SUBLANES, TQ), F32),
                        pltpu.VMEM((nh, TQ, TQ), F32),
                        pltpu.VMEM((nh, SUBLANES, TQ), F32)],
        compiler_params=pltpu.CompilerParams(
            dimension_semantics=("parallel", "parallel", "arbitrary"), vmem_limit_bytes=VMEM_LIMIT),
        name="fox_attention",
    )(qa, ka, vt)


def _gla_kernel(q_ref, k_ref, la_ref, v_ref, go_ref, ng_ref, o_ref, st_ref, obuf_ref):
    rows = q_ref.shape[0]

    @pl.when(pl.program_id(2) == 0)
    def _():
        st_ref[...] = jnp.zeros_like(st_ref)

    r = lax.broadcasted_iota(jnp.int32, (CHUNK, CHUNK), 0)
    c = lax.broadcasted_iota(jnp.int32, (CHUNK, CHUNK), 1)
    tri = jnp.where(r >= c, 1.0, 0.0).astype(BF16)
    vrow = lax.broadcasted_iota(jnp.int32, (2 * GLA_VAL_DIM, 2 * GLA_KEY_DIM), 0)
    kcol = lax.broadcasted_iota(jnp.int32, (2 * GLA_VAL_DIM, 2 * GLA_KEY_DIM), 1)
    same_head = (vrow >= GLA_VAL_DIM) == (kcol >= GLA_KEY_DIM)

    for ch in range(rows // CHUNK):
        sl = slice(ch * CHUNK, (ch + 1) * CHUNK)
        cum = _dot3_exact_lhs(tri, la_ref[sl, :])
        tot = cum[CHUNK - 1:CHUNK, :]
        kd = (k_ref[sl, :] * jnp.exp(tot - cum)).astype(BF16)
        upd_t = _dot_tn(v_ref[sl, :], kd)
        st = st_ref[...] * jnp.exp(tot) + jnp.where(same_head, upd_t, 0.0)
        st_ref[...] = st
        obuf_ref[sl, :] = _dot_nt(q_ref[sl, :], st.astype(BF16))

    o = obuf_ref[...]
    go = go_ref[...]
    ng = ng_ref[...]
    halves = []
    for h in range(2):
        hs = slice(h * GLA_VAL_DIM, (h + 1) * GLA_VAL_DIM)
        oh = o[:, hs]
        oh = oh * lax.rsqrt(jnp.mean(oh * oh, axis=-1, keepdims=True) + EPS) * ng
        gh = go[:, hs]
        halves.append(oh * (gh * jax.nn.sigmoid(gh)))
    o_ref[...] = jnp.concatenate(halves, axis=1).astype(o_ref.dtype)


def _gla(gq, gk, la, gv, go, ng, batch, seq):
    t = batch * seq
    nr = seq // GLA_ROWS
    pairs = GLA_HEADS // 2
    kspec = pl.BlockSpec((GLA_ROWS, 2 * GLA_KEY_DIM), lambda b, p, i: (b * nr + i, p))
    vspec = pl.BlockSpec((GLA_ROWS, 2 * GLA_VAL_DIM), lambda b, p, i: (b * nr + i, p))
    return pl.pallas_call(
        _gla_kernel,
        grid=(batch, pairs, nr),
        in_specs=[kspec, kspec, kspec, vspec, vspec,
                  pl.BlockSpec((1, GLA_VAL_DIM), lambda b, p, i: (0, 0))],
        out_specs=vspec,
        out_shape=jax.ShapeDtypeStruct((t, GLA_VAL_WIDTH), BF16),
        scratch_shapes=[pltpu.VMEM((2 * GLA_VAL_DIM, 2 * GLA_KEY_DIM), F32),
                        pltpu.VMEM((GLA_ROWS, 2 * GLA_VAL_DIM), F32)],
        compiler_params=pltpu.CompilerParams(
            dimension_semantics=("parallel", "parallel", "arbitrary"), vmem_limit_bytes=VMEM_LIMIT),
        name="gla",
    )(gq, gk, la, gv, go, ng)


def _out_proj_kernel(x_ref, fox_ref, gla_ref, wf_ref, wg_ref, g2_ref, rwh_ref, rwl_ref, rb_ref,
                     h_ref, n2_ref, ti_ref, gate_ref, rank_ref, cnt_ref, carry_ref):
    tm = x_ref.shape[0]

    @pl.when(pl.program_id(0) == 0)
    def _():
        carry_ref[...] = jnp.zeros_like(carry_ref)

    h = x_ref[...] + _dot(fox_ref[...], wf_ref[...]) + _dot(gla_ref[...], wg_ref[...])
    h_ref[...] = h
    n2 = _rms(h, g2_ref[...])
    for c in range(ROW_TILES):
        n2_ref[pl.ds(c, tm, stride=ROW_TILES), :] = n2[:, c * LANES:(c + 1) * LANES]

    logits = _dot_hi(n2, rwh_ref[...], rwl_ref[...]) + rb_ref[...]
    lane = lax.broadcasted_iota(jnp.int32, (tm, LANES), 1)
    work = logits
    vals, idxs = [], []
    for _ in range(TOP_K):
        m = jnp.max(work, axis=-1, keepdims=True)
        idx = jnp.min(jnp.where(work == m, lane, LANES), axis=-1, keepdims=True)
        vals.append(m)
        idxs.append(idx)
        work = jnp.where(lane == idx, -jnp.inf, work)
    exps = [jnp.exp(v - vals[0]) for v in vals]
    denom = exps[0] + exps[1] + exps[2] + exps[3]

    onehots = [lane == idx for idx in idxs]
    chosen = jnp.zeros((tm, LANES), F32)
    for oh in onehots:
        chosen = chosen + jnp.where(oh, 1.0, 0.0)
    row = lax.broadcasted_iota(jnp.int32, (tm, tm), 0)
    col = lax.broadcasted_iota(jnp.int32, (tm, tm), 1)
    strict = jnp.where(row > col, 1.0, 0.0).astype(BF16)
    before = _dot(strict, chosen.astype(BF16)) + carry_ref[...]
    carry = carry_ref[...] + jnp.sum(chosen, axis=0, keepdims=True)
    carry_ref[...] = carry
    cnt_ref[...] = jnp.broadcast_to(carry, cnt_ref.shape).astype(jnp.int32)

    ti = jnp.zeros((tm, LANES), jnp.int32)
    gates = jnp.zeros((tm, LANES), F32)
    ranks = jnp.zeros((tm, LANES), F32)
    for kk in range(TOP_K):
        sel = lane == kk
        ti = jnp.where(sel, idxs[kk], ti)
        gates = jnp.where(sel, exps[kk] / denom, gates)
        rk = jnp.sum(jnp.where(onehots[kk], before, 0.0), axis=-1, keepdims=True)
        ranks = jnp.where(sel, rk, ranks)
    ti_ref[...] = ti
    gate_ref[...] = gates
    rank_ref[...] = ranks.astype(jnp.int32)


def _out_proj(x2, fox, gla, wf, wg, g2, rw_hi, rw_lo, rb_pad):
    t = x2.shape[0]
    tm = TM_PROJ
    rows = lambda w: pl.BlockSpec((tm, w), lambda i: (i, 0))
    full = lambda a: pl.BlockSpec(a.shape, lambda i: (0,) * a.ndim)
    out_shape = [
        jax.ShapeDtypeStruct((t, D_MODEL), F32),
        jax.ShapeDtypeStruct((t * ROW_TILES, LANES), F32),
        jax.ShapeDtypeStruct((t, LANES), jnp.int32),
        jax.ShapeDtypeStruct((t, LANES), F32),
        jax.ShapeDtypeStruct((t, LANES), jnp.int32),
        jax.ShapeDtypeStruct((SUBLANES, LANES), jnp.int32),
    ]
    out_specs = [rows(D_MODEL), pl.BlockSpec((tm * ROW_TILES, LANES), lambda i: (i, 0)),
                 rows(LANES), rows(LANES), rows(LANES),
                 pl.BlockSpec((SUBLANES, LANES), lambda i: (0, 0))]
    return pl.pallas_call(
        _out_proj_kernel,
        grid=(t // tm,),
        in_specs=[rows(D_MODEL), rows(FOX_WIDTH), rows(GLA_VAL_WIDTH), full(wf), full(wg), full(g2),
                  full(rw_hi), full(rw_lo), full(rb_pad)],
        out_specs=out_specs,
        out_shape=out_shape,
        scratch_shapes=[pltpu.VMEM((1, LANES), F32)],
        compiler_params=pltpu.CompilerParams(
            dimension_semantics=("arbitrary",), vmem_limit_bytes=VMEM_LIMIT),
        name="out_proj_router",
    )(x2, fox, gla, wf, wg, g2, rw_hi, rw_lo, rb_pad)


def _row_copy(src_ref, src_row, dst_ref, dst_row, sem):
    return pltpu.make_async_copy(
        src_ref.at[pl.ds(pl.multiple_of(src_row * ROW_TILES, ROW_TILES), ROW_TILES)],
        dst_ref.at[pl.ds(pl.multiple_of(dst_row * ROW_TILES, ROW_TILES), ROW_TILES)],
        sem)


def _dispatch_kernel(pos_ref, n2_ref, xs_in_ref, xs_ref, sem):
    del xs_in_ref
    tb = pos_ref.shape[0] // TOP_K

    def issue(t, c):
        for kk in range(TOP_K):
            _row_copy(n2_ref, t, xs_ref, pos_ref[t * TOP_K + kk], sem).start(priority=kk % 2)
        return c

    lax.fori_loop(0, tb, issue, 0)
    for kk in range(TOP_K):
        pltpu.make_async_copy(n2_ref, xs_ref.at[pl.ds(0, tb * ROW_TILES)], sem).wait()


def _dispatch(pos_flat, n2_tiles, xs_init):
    t = n2_tiles.shape[0] // ROW_TILES
    tb = TB_DISPATCH
    return pl.pallas_call(
        _dispatch_kernel,
        grid=(t // tb,),
        in_specs=[pl.BlockSpec((tb * TOP_K,), lambda i: (i,), memory_space=pltpu.SMEM),
                  pl.BlockSpec((tb * ROW_TILES, LANES), lambda i: (i, 0)),
                  pl.BlockSpec(memory_space=pl.ANY)],
        out_specs=pl.BlockSpec(memory_space=pl.ANY),
        out_shape=jax.ShapeDtypeStruct(xs_init.shape, xs_init.dtype),
        scratch_shapes=[pltpu.SemaphoreType.DMA],
        input_output_aliases={2: 0},
        compiler_params=pltpu.CompilerParams(
            dimension_semantics=("arbitrary",), vmem_limit_bytes=VMEM_LIMIT),
        name="dispatch",
    )(pos_flat, n2_tiles, xs_init)


def _experts_kernel(te_ref, nu_ref, xs_ref, wi_ref, bi_ref, wo_ref, bo_ref, ys_ref, wib_ref, wob_ref):
    i = pl.program_id(0)
    tm = xs_ref.shape[0] // ROW_TILES
    prev = te_ref[jnp.maximum(i - 1, 0)]

    @pl.when((i == 0) | (te_ref[i] != prev))
    def _():
        wib_ref[...] = wi_ref[0].astype(BF16)
        wob_ref[...] = wo_ref[0].astype(BF16)

    @pl.when(i < nu_ref[0])
    def _():
        x = jnp.concatenate(
            [xs_ref[pl.ds(c, tm, stride=ROW_TILES), :] for c in range(ROW_TILES)], axis=1)
        h = _dot(x.astype(BF16), wib_ref[...]) + bi_ref[0]
        gate = jnp.minimum(h[:, :D_FF], SWIGLU_LIMIT)
        lin = jnp.clip(h[:, D_FF:], -SWIGLU_LIMIT, SWIGLU_LIMIT)
        a = (lin + 1.0) * (gate * jax.nn.sigmoid(SWIGLU_ALPHA * gate))
        y = _dot(a.astype(BF16), wob_ref[...]) + bo_ref[0]
        for c in range(ROW_TILES):
            ys_ref[pl.ds(c, tm, stride=ROW_TILES), :] = y[:, c * LANES:(c + 1) * LANES]

    @pl.when(i >= nu_ref[0])
    def _():
        ys_ref[...] = jnp.zeros_like(ys_ref)


def _experts(tile_expert, num_used, xs, w_in, b_in, w_out, b_out):
    n_tiles = tile_expert.shape[0]
    tm = TM_EXP
    grid_spec = pltpu.PrefetchScalarGridSpec(
        num_scalar_prefetch=2,
        grid=(n_tiles,),
        in_specs=[
            pl.BlockSpec((tm * ROW_TILES, LANES), lambda i, te, nu: (i, 0)),
            pl.BlockSpec((1, D_MODEL, 2 * D_FF), lambda i, te, nu: (te[i], 0, 0)),
            pl.BlockSpec((1, 1, 2 * D_FF), lambda i, te, nu: (te[i], 0, 0)),
            pl.BlockSpec((1, D_FF, D_MODEL), lambda i, te, nu: (te[i], 0, 0)),
            pl.BlockSpec((1, 1, D_MODEL), lambda i, te, nu: (te[i], 0, 0)),
        ],
        out_specs=pl.BlockSpec((tm * ROW_TILES, LANES), lambda i, te, nu: (i, 0)),
        scratch_shapes=[pltpu.VMEM((D_MODEL, 2 * D_FF), BF16), pltpu.VMEM((D_FF, D_MODEL), BF16)],
    )
    return pl.pallas_call(
        _experts_kernel,
        grid_spec=grid_spec,
        out_shape=jax.ShapeDtypeStruct(xs.shape, F32),
        compiler_params=pltpu.CompilerParams(
            dimension_semantics=("arbitrary",), vmem_limit_bytes=VMEM_LIMIT),
        name="experts",
    )(tile_expert, num_used, xs, w_in, b_in, w_out, b_out)


def _combine_kernel(pos_ref, gate_ref, h_ref, fg_ref, ys_ref, o_ref, gbuf_ref, sem):
    tc = h_ref.shape[0]

    def issue(t, c):
        for kk in range(TOP_K):
            _row_copy(ys_ref, pos_ref[t * TOP_K + kk], gbuf_ref.at[kk], t, sem).start(priority=kk % 2)
        return c

    lax.fori_loop(0, tc, issue, 0)
    for kk in range(TOP_K):
        pltpu.make_async_copy(ys_ref.at[pl.ds(0, tc * ROW_TILES)], gbuf_ref.at[kk], sem).wait()

    gates = gate_ref[...]
    h = h_ref[...]
    for kk in range(TOP_K):
        yk = jnp.concatenate(
            [gbuf_ref[kk, pl.ds(c, tc, stride=ROW_TILES), :] for c in range(ROW_TILES)], axis=1)
        h = h + gates[:, kk:kk + 1] * yk
    o_ref[...] = _rms(h, fg_ref[...])


def _combine(pos_flat, gates, h1, final_g, ys):
    t = h1.shape[0]
    tc = TC_COMBINE
    return pl.pallas_call(
        _combine_kernel,
        grid=(t // tc,),
        in_specs=[pl.BlockSpec((tc * TOP_K,), lambda i: (i,), memory_space=pltpu.SMEM),
                  pl.BlockSpec((tc, LANES), lambda i: (i, 0)),
                  pl.BlockSpec((tc, D_MODEL), lambda i: (i, 0)),
                  pl.BlockSpec((1, D_MODEL), lambda i: (0, 0)),
                  pl.BlockSpec(memory_space=pl.ANY)],
        out_specs=pl.BlockSpec((tc, D_MODEL), lambda i: (i, 0)),
        out_shape=jax.ShapeDtypeStruct((t, D_MODEL), F32),
        scratch_shapes=[pltpu.VMEM((TOP_K, tc * ROW_TILES, LANES), F32), pltpu.SemaphoreType.DMA],
        compiler_params=pltpu.CompilerParams(
            dimension_semantics=("arbitrary",), vmem_limit_bytes=VMEM_LIMIT),
        name="combine",
    )(pos_flat, gates, h1, final_g, ys)


def _hi_lo(w):
    hi = w.astype(BF16)
    return hi, (w - hi.astype(F32)).astype(BF16)


def _layer(x2, batch, seq, norm1_g, w_in, fox_f_bias, gla_gate_up, gla_gate_bias, gla_norm_g, w_out,
           norm2_g, router_w, router_b, exp_w_in, exp_b_in, exp_w_out, exp_b_out, final_g):
    t = batch * seq
    o = 0
    segs = {}
    for name, width in (("fq", FOX_WIDTH), ("fk", FOX_WIDTH), ("fv", FOX_WIDTH), ("ff", FOX_HEADS),
                        ("gq", GLA_KEY_WIDTH), ("gk", GLA_KEY_WIDTH), ("gv", GLA_VAL_WIDTH),
                        ("gl", GLA_GATE_RANK), ("go", GLA_VAL_WIDTH)):
        segs[name] = w_in[:, o:o + width]
        o += width
    w_main = jnp.concatenate(
        [segs[n] for n in ("fq", "fk", "fv", "gq", "gk", "gv", "go")], axis=1).astype(BF16)
    n_small = FOX_HEADS + GLA_GATE_RANK
    w_small = jnp.concatenate(
        [segs["ff"], segs["gl"], segs["ff"], segs["ff"],
         jnp.zeros((D_MODEL, LANES - n_small - 2 * FOX_HEADS), F32)], axis=1)
    ws_hi, ws_lo = _hi_lo(w_small)
    fb_pad = jnp.concatenate(
        [fox_f_bias, jnp.zeros((GLA_GATE_RANK,), F32), fox_f_bias, fox_f_bias,
         jnp.zeros((LANES - n_small - 2 * FOX_HEADS,), F32)]).reshape(1, LANES)
    gu_pad = jnp.pad(gla_gate_up, ((FOX_HEADS, LANES - n_small), (0, 0)))
    gu_hi, gu_lo = _hi_lo(gu_pad)

    qa, ka, vt, gq, gk, gv, go, la = _in_proj(
        x2, norm1_g.reshape(1, D_MODEL), w_main, ws_hi, ws_lo, fb_pad, gu_hi, gu_lo,
        gla_gate_bias.reshape(1, GLA_KEY_WIDTH), batch, seq)

    fox = _fox(qa, ka, vt, batch, seq)
    gla = _gla(gq, gk, la, gv, go, gla_norm_g.reshape(1, GLA_VAL_DIM), batch, seq)

    w_out_b = w_out.astype(BF16)
    rw_hi, rw_lo = _hi_lo(jnp.pad(router_w, ((0, 0), (0, LANES - N_EXPERTS))))
    rb_pad = jnp.pad(router_b, (0, LANES - N_EXPERTS), constant_values=NEG_BIG).reshape(1, LANES)
    h1, n2_tiles, ti, gates, rank, counts = _out_proj(
        x2, fox, gla, w_out_b[:FOX_WIDTH], w_out_b[FOX_WIDTH:], norm2_g.reshape(1, D_MODEL),
        rw_hi, rw_lo, rb_pad)

    cnt = counts[0, :N_EXPERTS]
    padded = ((cnt + TM_EXP - 1) // TM_EXP) * TM_EXP
    ends = jnp.cumsum(padded)
    starts = ends - padded
    pos = (starts[ti[:, :TOP_K]] + rank[:, :TOP_K]).reshape(-1).astype(jnp.int32)
    n_tiles = (t * TOP_K) // TM_EXP + N_EXPERTS
    num_used = (ends[-1] // TM_EXP).astype(jnp.int32)
    tile_start = jnp.arange(n_tiles, dtype=jnp.int32) * TM_EXP
    tile_expert = jnp.minimum(
        jnp.sum((tile_start[:, None] >= ends[None, :]).astype(jnp.int32), axis=1), N_EXPERTS - 1)
    last_expert = tile_expert[jnp.maximum(num_used - 1, 0)]
    tile_expert = jnp.where(jnp.arange(n_tiles) < num_used, tile_expert, last_expert).astype(jnp.int32)

    xs_init = jnp.zeros((n_tiles * TM_EXP * ROW_TILES, LANES), F32)
    xs = _dispatch(pos, n2_tiles, xs_init)
    ys = _experts(tile_expert, num_used.reshape(1), xs, exp_w_in,
                  exp_b_in.reshape(N_EXPERTS, 1, 2 * D_FF), exp_w_out,
                  exp_b_out.reshape(N_EXPERTS, 1, D_MODEL))
    return _combine(pos, gates, h1, final_g.reshape(1, D_MODEL), ys)


def kernel(x, norm1_g, w_in, fox_f_bias, gla_gate_up, gla_gate_bias, gla_norm_g, w_out, norm2_g,
           router_w, router_b, exp_w_in, exp_b_in, exp_w_out, exp_b_out, final_g):
    batch, seq, d = x.shape
    depth = norm1_g.shape[0]
    assert depth == 1 and d == D_MODEL
    out = _layer(x.reshape(batch * seq, d), batch, seq, norm1_g[0], w_in[0], fox_f_bias[0],
                 gla_gate_up[0], gla_gate_bias[0], gla_norm_g[0], w_out[0], norm2_g[0], router_w[0],
                 router_b[0], exp_w_in[0], exp_b_in[0], exp_w_out[0], exp_b_out[0], final_g)
    return out.reshape(batch, seq, d)
```

```python
import jax
import jax.numpy as jnp
import numpy as np
from jax import lax
from jax.experimental import pallas as pl
from jax.experimental.pallas import tpu as pltpu

F32 = jnp.float32
BF16 = jnp.bfloat16

D_MODEL = 1024
FOX_HEADS = 8
FOX_HEAD_DIM = 64
FOX_WIDTH = 512
GLA_HEADS = 4
GLA_KEY_DIM = 64
GLA_KEY_WIDTH = 256
GLA_VAL_DIM = 128
GLA_VAL_WIDTH = 512
GLA_GATE_RANK = 16
GLA_GATE_TAU = 16.0
CHUNK = 64
N_EXPERTS = 32
TOP_K = 4
D_FF = 1024
SWIGLU_LIMIT = 7.0
SWIGLU_ALPHA = 1.702
EPS = 1e-5

LANES = 128
SUBLANES = 8
ROW_TILES = D_MODEL // LANES
VMEM_LIMIT = 56 * 1024 * 1024

TM_PROJ = 256
TQ = 256
FOX_HEADS_PER_STEP = 4
GLA_ROWS = 512
TM_EXP = 256
SLOT_BLOCK = 8192
SCALAR_UNROLL = 8
TC_COMBINE = 256
QK_WIDTH = FOX_HEADS * LANES
F_COPIES = (0, FOX_HEADS + GLA_GATE_RANK, 2 * FOX_HEADS + GLA_GATE_RANK)
F_COPY2, F_COPY3 = F_COPIES[1:]
VT_ROWS = FOX_HEAD_DIM + 16
NEG_BIG = -1e30
LOG2E = 1.4426950408889634


def _log_sigmoid(z):
    return jnp.minimum(z, 0.0) - jnp.log1p(jnp.exp(-jnp.abs(z)))


def _split3(a):
    p1 = a.astype(BF16)
    r1 = a - p1.astype(F32)
    p2 = r1.astype(BF16)
    r2 = r1 - p2.astype(F32)
    return p1, p2, r2.astype(BF16)


def _dot(a, b):
    return jnp.dot(a, b, preferred_element_type=F32)


def _dot_nt(a, b):
    return lax.dot_general(a, b, (((1,), (1,)), ((), ())), preferred_element_type=F32)


def _dot_tn(a, b):
    return lax.dot_general(a, b, (((0,), (0,)), ((), ())), preferred_element_type=F32)


def _dot3_exact_lhs(tri, a):
    p1, p2, p3 = _split3(a)
    return _dot(tri, p1) + _dot(tri, p2) + _dot(tri, p3)


def _dot_hi(a, b_hi, b_lo):
    a_hi = a.astype(BF16)
    a_lo = (a - a_hi.astype(F32)).astype(BF16)
    return _dot(a_hi, b_hi) + _dot(a_lo, b_hi) + _dot(a_hi, b_lo)


def _rms(x, g):
    return x * lax.rsqrt(jnp.mean(x * x, axis=-1, keepdims=True) + EPS) * g


def _in_proj_kernel(x_ref, g_ref, wm_ref, wsh_ref, wsl_ref, fb_ref, guh_ref, gul_ref, gb_ref,
                    selq_ref, selk_ref, oneq_ref, onek_ref,
                    qa_ref, ka_ref, vt_ref, gq_ref, gk_ref, gv_ref, go_ref, la_ref, carry_ref):
    tm = x_ref.shape[0]

    @pl.when(pl.program_id(1) == 0)
    def _():
        carry_ref[...] = jnp.zeros_like(carry_ref)

    n = _rms(x_ref[...], g_ref[...])
    main = _dot(n.astype(BF16), wm_ref[...])

    small = _dot_hi(n, wsh_ref[...], wsl_ref[...])

    ls = _log_sigmoid(small + fb_ref[...])
    row = lax.broadcasted_iota(jnp.int32, (tm, tm), 0)
    col = lax.broadcasted_iota(jnp.int32, (tm, tm), 1)
    tri = jnp.where(row >= col, 1.0, 0.0).astype(BF16)
    cum = _dot3_exact_lhs(tri, ls) + carry_ref[...]
    carry_ref[...] = cum[tm - 1:tm, :]

    lane = lax.broadcasted_iota(jnp.int32, (tm, LANES), 1)
    p1, p2, p3 = _split3(cum * LOG2E)
    zero = jnp.zeros_like(p1)
    comb = jnp.where(lane < FOX_HEADS, p1,
                     jnp.where((lane >= F_COPY2) & (lane < F_COPY2 + FOX_HEADS), p2,
                               jnp.where((lane >= F_COPY3) & (lane < F_COPY3 + FOX_HEADS), p3, zero)))
    aug_q = _dot(comb, selq_ref[...]) + oneq_ref[...]
    aug_k = _dot(comb, selk_ref[...]) + onek_ref[...]
    lane1 = lax.broadcasted_iota(jnp.int32, (1, LANES), 1)
    q_scale = jnp.where(lane1 < FOX_HEAD_DIM, LOG2E * FOX_HEAD_DIM ** -0.5, 0.0)
    k_scale = jnp.where(lane1 < FOX_HEAD_DIM, 1.0, 0.0)
    ones_row = jnp.where(lax.broadcasted_iota(jnp.int32, (VT_ROWS - FOX_HEAD_DIM, tm), 0) == 0, 1.0, 0.0)
    for p in range(FOX_HEADS // 2):
        qc = main[:, p * LANES:(p + 1) * LANES]
        kc = main[:, FOX_WIDTH + p * LANES:FOX_WIDTH + (p + 1) * LANES]
        vc_t = main[:, 2 * FOX_WIDTH + p * LANES:2 * FOX_WIDTH + (p + 1) * LANES].T
        for hh in range(2):
            h = 2 * p + hh
            hs = slice(h * LANES, (h + 1) * LANES)
            if hh:
                qc = pltpu.roll(qc, FOX_HEAD_DIM, 1)
                kc = pltpu.roll(kc, FOX_HEAD_DIM, 1)
            qa_ref[:, hs] = (qc * q_scale + aug_q[:, hs]).astype(BF16)
            ka_ref[:, hs] = (kc * k_scale + aug_k[:, hs]).astype(BF16)
            vt_ref[h] = jnp.concatenate(
                [vc_t[hh * FOX_HEAD_DIM:(hh + 1) * FOX_HEAD_DIM], ones_row], axis=0).astype(BF16)
    o = 3 * FOX_WIDTH
    gq_ref[...] = (main[:, o:o + GLA_KEY_WIDTH] * (GLA_KEY_DIM ** -0.5)).astype(BF16); o += GLA_KEY_WIDTH
    gk_ref[...] = main[:, o:o + GLA_KEY_WIDTH]; o += GLA_KEY_WIDTH
    gv_ref[...] = main[:, o:o + GLA_VAL_WIDTH].astype(BF16); o += GLA_VAL_WIDTH
    go_ref[...] = main[:, o:o + GLA_VAL_WIDTH]

    z = _dot_hi(small, guh_ref[...], gul_ref[...]) + gb_ref[...]
    la_ref[...] = _log_sigmoid(z) * (1.0 / GLA_GATE_TAU)


def _aug_constants():
    selq = np.zeros((LANES, QK_WIDTH), np.float32)
    selk = np.zeros((LANES, QK_WIDTH), np.float32)
    oneq = np.zeros((1, QK_WIDTH), np.float32)
    onek = np.zeros((1, QK_WIDTH), np.float32)
    for h in range(FOX_HEADS):
        base = h * LANES + FOX_HEAD_DIM
        for i, copy in enumerate(F_COPIES):
            selq[copy + h, base + i] = 1.0
            onek[0, base + i] = 1.0
            selk[copy + h, base + 3 + i] = -1.0
            oneq[0, base + 3 + i] = 1.0
    return (jnp.asarray(selq, BF16), jnp.asarray(selk, BF16), jnp.asarray(oneq), jnp.asarray(onek))


def _in_proj(x2, norm_g, w_main, ws_hi, ws_lo, fb_pad, gu_hi, gu_lo, gb, batch, seq):
    t = batch * seq
    tm = TM_PROJ
    nj = seq // tm
    pairs = FOX_HEADS // 2
    selq, selk, oneq, onek = _aug_constants()
    rows = lambda w: pl.BlockSpec((tm, w), lambda b, j: (b * nj + j, 0))
    full = lambda a: pl.BlockSpec(a.shape, lambda b, j: (0,) * a.ndim)
    out_shape = [
        jax.ShapeDtypeStruct((t, QK_WIDTH), BF16),
        jax.ShapeDtypeStruct((t, QK_WIDTH), BF16),
        jax.ShapeDtypeStruct((FOX_HEADS, VT_ROWS, t), BF16),
        jax.ShapeDtypeStruct((t, GLA_KEY_WIDTH), BF16),
        jax.ShapeDtypeStruct((t, GLA_KEY_WIDTH), F32),
        jax.ShapeDtypeStruct((t, GLA_VAL_WIDTH), BF16),
        jax.ShapeDtypeStruct((t, GLA_VAL_WIDTH), F32),
        jax.ShapeDtypeStruct((t, GLA_KEY_WIDTH), F32),
    ]
    out_specs = [rows(QK_WIDTH), rows(QK_WIDTH),
                 pl.BlockSpec((FOX_HEADS, VT_ROWS, tm), lambda b, j: (0, 0, b * nj + j)),
                 rows(GLA_KEY_WIDTH), rows(GLA_KEY_WIDTH), rows(GLA_VAL_WIDTH), rows(GLA_VAL_WIDTH),
                 rows(GLA_KEY_WIDTH)]
    args = (x2, norm_g, w_main, ws_hi, ws_lo, fb_pad, gu_hi, gu_lo, gb, selq, selk, oneq, onek)
    return pl.pallas_call(
        _in_proj_kernel,
        grid=(batch, nj),
        in_specs=[rows(D_MODEL)] + [full(a) for a in args[1:]],
        out_specs=out_specs,
        out_shape=out_shape,
        scratch_shapes=[pltpu.VMEM((1, LANES), F32)],
        compiler_params=pltpu.CompilerParams(
            dimension_semantics=("arbitrary", "arbitrary"), vmem_limit_bytes=VMEM_LIMIT),
        name="in_proj",
    )(*args)


def _fox_kernel(q_ref, k_ref, vt_ref, o_ref, acc_ref, m_ref, sa_ref, ma_ref, sb_ref, mb_ref):
    tq = q_ref.shape[0]
    tk = tq
    nh = q_ref.shape[1] // LANES
    i = pl.program_id(2)

    m_ref[...] = jnp.full_like(m_ref, -jnp.inf)
    acc_ref[...] = jnp.zeros_like(acc_ref)
    slots = ((sa_ref, ma_ref), (sb_ref, mb_ref))

    def stage_scores(block, slot, masked=False):
        start = pl.multiple_of(block * tk, tk)
        for h in range(nh):
            s = _dot_nt(k_ref[pl.ds(start, tk), h * LANES:(h + 1) * LANES],
                        q_ref[:, h * LANES:(h + 1) * LANES])
            if masked:
                key = lax.broadcasted_iota(jnp.int32, (tk, tq), 0)
                qry = lax.broadcasted_iota(jnp.int32, (tk, tq), 1)
                s = jnp.where(key <= qry, s, -jnp.inf)
            slots[slot][0][h] = s
            slots[slot][1][h] = jnp.max(s.reshape(tk // SUBLANES, SUBLANES, tq), axis=0)

    def stage_accumulate(block, slot):
        start = pl.multiple_of(block * tk, tk)
        probs, alphas = [], []
        for h in range(nh):
            m_prev = m_ref[h]
            m_new = jnp.maximum(m_prev, jnp.max(slots[slot][1][h], axis=0, keepdims=True))
            probs.append(jnp.exp2(slots[slot][0][h] - m_new[0:1]).astype(BF16))
            alphas.append(jnp.exp2(m_prev - m_new)[0:1])
            m_ref[h] = m_new
        for h in range(nh):
            acc_ref[h] = alphas[h] * acc_ref[h] + _dot(vt_ref[h, :, pl.ds(start, tk)], probs[h])

    def processed(m):
        return jnp.where(m == 0, i, m - 1)

    stage_scores(i, 0, masked=True)
    pairs = i // 2

    def body(mm, c):
        m = 2 * mm
        stage_scores(m, 1)
        stage_accumulate(processed(m), 0)
        stage_scores(m + 1, 0)
        stage_accumulate(m, 1)
        return c

    lax.fori_loop(0, pairs, body, 0)
    last = 2 * pairs

    @pl.when(i > last)
    def _():
        stage_scores(last, 1)
        stage_accumulate(processed(last), 0)
        stage_accumulate(last, 1)

    @pl.when(i == last)
    def _():
        stage_accumulate(processed(last), 0)

    o_t = jnp.concatenate(
        [acc_ref[h, 0:FOX_HEAD_DIM, :] / acc_ref[h, FOX_HEAD_DIM:FOX_HEAD_DIM + 1, :]
         for h in range(nh)], axis=0)
    o_ref[...] = o_t.T.astype(o_ref.dtype)


def _fox(qa, ka, vt, batch, seq):
    t = batch * seq
    nq = seq // TQ
    nh = FOX_HEADS_PER_STEP
    groups = FOX_HEADS // nh
    return pl.pallas_call(
        _fox_kernel,
        grid=(batch, groups, nq),
        in_specs=[
            pl.BlockSpec((TQ, nh * LANES), lambda b, g, i: (b * nq + i, g)),
            pl.BlockSpec((seq, nh * LANES), lambda b, g, i: (b, g)),
            pl.BlockSpec((nh, VT_ROWS, seq), lambda b, g, i: (g, 0, b)),
        ],
        out_specs=pl.BlockSpec((TQ, nh * FOX_HEAD_DIM), lambda b, g, i: (b * nq + i, g)),
        out_shape=jax.ShapeDtypeStruct((t, FOX_WIDTH), BF16),
        scratch_shapes=[pltpu.VMEM((nh, VT_ROWS, TQ), F32),
                        pltpu.VMEM((nh, SUBLANES, TQ), F32),
                        pltpu.VMEM((nh, TQ, TQ), F32),
                        pltpu.VMEM((nh, SUBLANES, TQ), F32),
                        pltpu.VMEM((nh, TQ, TQ), F32),
                        pltpu.VMEM((nh, SUBLANES, TQ), F32)],
        compiler_params=pltpu.CompilerParams(
            dimension_semantics=("parallel", "parallel", "arbitrary"), vmem_limit_bytes=VMEM_LIMIT),
        name="fox_attention",
    )(qa, ka, vt)


def _gla_kernel(q_ref, k_ref, la_ref, v_ref, go_ref, ng_ref, o_ref, st_ref, obuf_ref):
    rows = q_ref.shape[0]

    @pl.when(pl.program_id(2) == 0)
    def _():
        st_ref[...] = jnp.zeros_like(st_ref)

    r = lax.broadcasted_iota(jnp.int32, (CHUNK, CHUNK), 0)
    c = lax.broadcasted_iota(jnp.int32, (CHUNK, CHUNK), 1)
    tri = jnp.where(r >= c, 1.0, 0.0).astype(BF16)
    vrow = lax.broadcasted_iota(jnp.int32, (2 * GLA_VAL_DIM, 2 * GLA_KEY_DIM), 0)
    kcol = lax.broadcasted_iota(jnp.int32, (2 * GLA_VAL_DIM, 2 * GLA_KEY_DIM), 1)
    same_head = (vrow >= GLA_VAL_DIM) == (kcol >= GLA_KEY_DIM)

    for ch in range(rows // CHUNK):
        sl = slice(ch * CHUNK, (ch + 1) * CHUNK)
        cum = _dot3_exact_lhs(tri, la_ref[sl, :])
        tot = cum[CHUNK - 1:CHUNK, :]
        kd = (k_ref[sl, :] * jnp.exp(tot - cum)).astype(BF16)
        upd_t = _dot_tn(v_ref[sl, :], kd)
        st = st_ref[...] * jnp.exp(tot) + jnp.where(same_head, upd_t, 0.0)
        st_ref[...] = st
        obuf_ref[sl, :] = _dot_nt(q_ref[sl, :], st.astype(BF16))

    o = obuf_ref[...]
    go = go_ref[...]
    ng = ng_ref[...]
    halves = []
    for h in range(2):
        hs = slice(h * GLA_VAL_DIM, (h + 1) * GLA_VAL_DIM)
        oh = o[:, hs]
        oh = oh * lax.rsqrt(jnp.mean(oh * oh, axis=-1, keepdims=True) + EPS) * ng
        gh = go[:, hs]
        halves.append(oh * (gh * jax.nn.sigmoid(gh)))
    o_ref[...] = jnp.concatenate(halves, axis=1).astype(o_ref.dtype)


def _gla(gq, gk, la, gv, go, ng, batch, seq):
    t = batch * seq
    nr = seq // GLA_ROWS
    pairs = GLA_HEADS // 2
    kspec = pl.BlockSpec((GLA_ROWS, 2 * GLA_KEY_DIM), lambda b, p, i: (b * nr + i, p))
    vspec = pl.BlockSpec((GLA_ROWS, 2 * GLA_VAL_DIM), lambda b, p, i: (b * nr + i, p))
    return pl.pallas_call(
        _gla_kernel,
        grid=(batch, pairs, nr),
        in_specs=[kspec, kspec, kspec, vspec, vspec,
                  pl.BlockSpec((1, GLA_VAL_DIM), lambda b, p, i: (0, 0))],
        out_specs=vspec,
        out_shape=jax.ShapeDtypeStruct((t, GLA_VAL_WIDTH), BF16),
        scratch_shapes=[pltpu.VMEM((2 * GLA_VAL_DIM, 2 * GLA_KEY_DIM), F32),
                        pltpu.VMEM((GLA_ROWS, 2 * GLA_VAL_DIM), F32)],
        compiler_params=pltpu.CompilerParams(
            dimension_semantics=("parallel", "parallel", "arbitrary"), vmem_limit_bytes=VMEM_LIMIT),
        name="gla",
    )(gq, gk, la, gv, go, ng)


def _out_proj_kernel(x_ref, fox_ref, gla_ref, wf_ref, wg_ref, g2_ref, rwh_ref, rwl_ref, rb_ref,
                     h_ref, n2_ref, ti_ref, gate_ref, rank_ref, cnt_ref, carry_ref):
    tm = x_ref.shape[0]

    @pl.when(pl.program_id(0) == 0)
    def _():
        carry_ref[...] = jnp.zeros_like(carry_ref)

    h = x_ref[...] + _dot(fox_ref[...], wf_ref[...]) + _dot(gla_ref[...], wg_ref[...])
    h_ref[...] = h
    n2 = _rms(h, g2_ref[...])
    for c in range(ROW_TILES):
        n2_ref[pl.ds(c, tm, stride=ROW_TILES), :] = n2[:, c * LANES:(c + 1) * LANES]

    logits = _dot_hi(n2, rwh_ref[...], rwl_ref[...]) + rb_ref[...]
    lane = lax.broadcasted_iota(jnp.int32, (tm, LANES), 1)
    work = logits
    vals, idxs = [], []
    for _ in range(TOP_K):
        m = jnp.max(work, axis=-1, keepdims=True)
        idx = jnp.min(jnp.where(work == m, lane, LANES), axis=-1, keepdims=True)
        vals.append(m)
        idxs.append(idx)
        work = jnp.where(lane == idx, -jnp.inf, work)
    exps = [jnp.exp(v - vals[0]) for v in vals]
    denom = exps[0] + exps[1] + exps[2] + exps[3]

    onehots = [lane == idx for idx in idxs]
    chosen = jnp.zeros((tm, LANES), F32)
    for oh in onehots:
        chosen = chosen + jnp.where(oh, 1.0, 0.0)
    row = lax.broadcasted_iota(jnp.int32, (tm, tm), 0)
    col = lax.broadcasted_iota(jnp.int32, (tm, tm), 1)
    strict = jnp.where(row > col, 1.0, 0.0).astype(BF16)
    before = _dot(strict, chosen.astype(BF16)) + carry_ref[...]
    carry = carry_ref[...] + jnp.sum(chosen, axis=0, keepdims=True)
    carry_ref[...] = carry
    cnt_ref[...] = jnp.broadcast_to(carry, cnt_ref.shape).astype(jnp.int32)

    ti = jnp.zeros((tm, LANES), jnp.int32)
    gates = jnp.zeros((tm, LANES), F32)
    ranks = jnp.zeros((tm, LANES), F32)
    for kk in range(TOP_K):
        sel = lane == kk
        ti = jnp.where(sel, idxs[kk], ti)
        gates = jnp.where(sel, exps[kk] / denom, gates)
        rk = jnp.sum(jnp.where(onehots[kk], before, 0.0), axis=-1, keepdims=True)
        ranks = jnp.where(sel, rk, ranks)
    ti_ref[...] = ti
    gate_ref[...] = gates
    rank_ref[...] = ranks.astype(jnp.int32)


def _out_proj(x2, fox, gla, wf, wg, g2, rw_hi, rw_lo, rb_pad):
    t = x2.shape[0]
    tm = TM_PROJ
    rows = lambda w: pl.BlockSpec((tm, w), lambda i: (i, 0))
    full = lambda a: pl.BlockSpec(a.shape, lambda i: (0,) * a.ndim)
    out_shape = [
        jax.ShapeDtypeStruct((t, D_MODEL), F32),
        jax.ShapeDtypeStruct((t * ROW_TILES, LANES), F32),
        jax.ShapeDtypeStruct((t, LANES), jnp.int32),
        jax.ShapeDtypeStruct((t, LANES), F32),
        jax.ShapeDtypeStruct((t, LANES), jnp.int32),
        jax.ShapeDtypeStruct((SUBLANES, LANES), jnp.int32),
    ]
    out_specs = [rows(D_MODEL), pl.BlockSpec((tm * ROW_TILES, LANES), lambda i: (i, 0)),
                 rows(LANES), rows(LANES), rows(LANES),
                 pl.BlockSpec((SUBLANES, LANES), lambda i: (0, 0))]
    return pl.pallas_call(
        _out_proj_kernel,
        grid=(t // tm,),
        in_specs=[rows(D_MODEL), rows(FOX_WIDTH), rows(GLA_VAL_WIDTH), full(wf), full(wg), full(g2),
                  full(rw_hi), full(rw_lo), full(rb_pad)],
        out_specs=out_specs,
        out_shape=out_shape,
        scratch_shapes=[pltpu.VMEM((1, LANES), F32)],
        compiler_params=pltpu.CompilerParams(
            dimension_semantics=("arbitrary",), vmem_limit_bytes=VMEM_LIMIT),
        name="out_proj_router",
    )(x2, fox, gla, wf, wg, g2, rw_hi, rw_lo, rb_pad)


def _slot_sources_kernel(starts_ref, cnt_ref, ends_ref, pos_ref, src_ref):
    g = pl.program_id(0)
    nblk = pos_ref.shape[0]
    n_slots = src_ref.shape[0]
    n_pairs = nblk * pl.num_programs(0)

    @pl.when(g == 0)
    def _():
        def fill(lo, hi, dump):
            def body(r, d):
                src_ref[r] = d
                return d + 1
            return lax.fori_loop(lo, hi, body, dump)

        def per_expert(e, dump):
            return fill(starts_ref[e] + cnt_ref[e], ends_ref[e], dump)

        dump = lax.fori_loop(0, N_EXPERTS, per_expert, jnp.int32(n_pairs))
        fill(ends_ref[N_EXPERTS - 1], n_slots, dump)

    base = g * nblk

    def body(j, c):
        for u in range(SCALAR_UNROLL):
            p = j * SCALAR_UNROLL + u
            src_ref[pos_ref[p]] = base + p
        return c

    lax.fori_loop(0, nblk // SCALAR_UNROLL, body, 0)


def _slot_sources(starts, cnt, ends, pos_km, n_slots):
    n_pairs = pos_km.shape[0]
    blk = SLOT_BLOCK
    grid_spec = pltpu.PrefetchScalarGridSpec(
        num_scalar_prefetch=3,
        grid=(n_pairs // blk,),
        in_specs=[pl.BlockSpec((blk,), lambda g, s, c, e: (g,), memory_space=pltpu.SMEM)],
        out_specs=pl.BlockSpec((n_slots,), lambda g, s, c, e: (0,), memory_space=pltpu.SMEM),
    )
    return pl.pallas_call(
        _slot_sources_kernel,
        grid_spec=grid_spec,
        out_shape=jax.ShapeDtypeStruct((n_slots,), jnp.int32),
        compiler_params=pltpu.CompilerParams(dimension_semantics=("arbitrary",)),
        name="slot_sources",
    )(starts, cnt, ends, pos_km)


def _tile_rows(ref, row):
    return ref.at[pl.ds(pl.multiple_of(row * ROW_TILES, ROW_TILES), ROW_TILES)]


def _experts_kernel(te_ref, nu_ref, src_ref, n2_ref, wi_ref, bi_ref, wo_ref, bo_ref, yg_ref,
                    wib_ref, wob_ref, xbuf_ref, ybuf_ref, gsem, ssem):
    i = pl.program_id(0)
    n_tiles = pl.num_programs(0)
    tm = xbuf_ref.shape[1] // ROW_TILES
    n_tokens = n2_ref.shape[0] // ROW_TILES
    nu = nu_ref[0]

    def issue_gather(tile, slot):
        base = jnp.minimum(tile, n_tiles - 1) * tm
        for r in range(tm):
            token = src_ref[base + r] & (n_tokens - 1)
            pltpu.make_async_copy(
                _tile_rows(n2_ref, token), xbuf_ref.at[slot, pl.ds(r * ROW_TILES, ROW_TILES)],
                gsem.at[slot]).start(priority=r % 2)

    def wait_gather(slot):
        pltpu.make_async_copy(
            n2_ref.at[pl.ds(0, tm * ROW_TILES)], xbuf_ref.at[slot], gsem.at[slot]).wait()

    def issue_scatter(tile, slot):
        base = tile * tm
        for r in range(tm):
            pltpu.make_async_copy(
                ybuf_ref.at[slot, pl.ds(r * ROW_TILES, ROW_TILES)],
                _tile_rows(yg_ref, src_ref[base + r]), ssem.at[slot]).start(priority=r % 2)

    def wait_scatter(slot):
        pltpu.make_async_copy(
            ybuf_ref.at[slot], yg_ref.at[pl.ds(0, tm * ROW_TILES)], ssem.at[slot]).wait()

    @pl.when(i == 0)
    def _():
        issue_gather(0, 0)
        issue_gather(1, 1)

    prev = te_ref[jnp.maximum(i - 1, 0)]

    @pl.when((i == 0) | (te_ref[i] != prev))
    def _():
        wib_ref[...] = wi_ref[0].astype(BF16)
        wob_ref[...] = wo_ref[0].astype(BF16)

    yslot = lax.rem(i, 2)

    @pl.when((i >= 2) & (i < nu))
    def _():
        wait_scatter(yslot)

    @pl.when(i < nu)
    def _():
        xslot = lax.rem(i, 3)
        wait_gather(xslot)
        issue_gather(i + 2, lax.rem(i + 2, 3))
        x = jnp.concatenate(
            [xbuf_ref[xslot, pl.ds(c, tm, stride=ROW_TILES), :] for c in range(ROW_TILES)], axis=1)
        h = _dot(x.astype(BF16), wib_ref[...]) + bi_ref[0]
        gate = jnp.minimum(h[:, :D_FF], SWIGLU_LIMIT)
        lin = jnp.clip(h[:, D_FF:], -SWIGLU_LIMIT, SWIGLU_LIMIT)
        a = (lin + 1.0) * (gate * jax.nn.sigmoid(SWIGLU_ALPHA * gate))
        y = _dot(a.astype(BF16), wob_ref[...]) + bo_ref[0]
        for c in range(ROW_TILES):
            ybuf_ref[yslot, pl.ds(c, tm, stride=ROW_TILES), :] = y[:, c * LANES:(c + 1) * LANES]
        issue_scatter(i, yslot)

    @pl.when(i == nu - 1)
    def _():
        wait_gather(lax.rem(i + 1, 3))
        wait_gather(lax.rem(i + 2, 3))
        wait_scatter(yslot)

        @pl.when(i >= 1)
        def _():
            wait_scatter(1 - yslot)

    @pl.when(i >= nu)
    def _():
        ybuf_ref[0] = jnp.zeros_like(ybuf_ref[0])
        tail = pltpu.make_async_copy(
            ybuf_ref.at[0], yg_ref.at[pl.ds(pl.multiple_of(src_ref[i * tm] * ROW_TILES, ROW_TILES),
                                            tm * ROW_TILES)], ssem.at[0])
        tail.start()
        tail.wait()


def _experts(tile_expert, num_used, src, n2_tiles, w_in, b_in, w_out, b_out):
    n_tiles = tile_expert.shape[0]
    tm = TM_EXP
    grid_spec = pltpu.PrefetchScalarGridSpec(
        num_scalar_prefetch=3,
        grid=(n_tiles,),
        in_specs=[
            pl.BlockSpec(memory_space=pl.ANY),
            pl.BlockSpec((1, D_MODEL, 2 * D_FF), lambda i, te, nu, src: (te[i], 0, 0)),
            pl.BlockSpec((1, 1, 2 * D_FF), lambda i, te, nu, src: (te[i], 0, 0)),
            pl.BlockSpec((1, D_FF, D_MODEL), lambda i, te, nu, src: (te[i], 0, 0)),
            pl.BlockSpec((1, 1, D_MODEL), lambda i, te, nu, src: (te[i], 0, 0)),
        ],
        out_specs=pl.BlockSpec(memory_space=pl.ANY),
        scratch_shapes=[pltpu.VMEM((D_MODEL, 2 * D_FF), BF16), pltpu.VMEM((D_FF, D_MODEL), BF16),
                        pltpu.VMEM((3, tm * ROW_TILES, LANES), F32),
                        pltpu.VMEM((2, tm * ROW_TILES, LANES), F32),
                        pltpu.SemaphoreType.DMA((3,)), pltpu.SemaphoreType.DMA((2,))],
    )
    return pl.pallas_call(
        _experts_kernel,
        grid_spec=grid_spec,
        out_shape=jax.ShapeDtypeStruct((n_tiles * tm * ROW_TILES, LANES), F32),
        compiler_params=pltpu.CompilerParams(
            dimension_semantics=("arbitrary",), vmem_limit_bytes=VMEM_LIMIT),
        name="experts",
    )(tile_expert, num_used, src, n2_tiles, w_in, b_in, w_out, b_out)


def _combine_kernel(gate_ref, h_ref, fg_ref, y0_ref, y1_ref, y2_ref, y3_ref, o_ref):
    tc = h_ref.shape[0]
    gates = gate_ref[...]
    h = h_ref[...]
    for kk, y_ref in enumerate((y0_ref, y1_ref, y2_ref, y3_ref)):
        yk = jnp.concatenate(
            [y_ref[pl.ds(c, tc, stride=ROW_TILES), :] for c in range(ROW_TILES)], axis=1)
        h = h + gates[:, kk:kk + 1] * yk
    o_ref[...] = _rms(h, fg_ref[...])


def _combine(gates, h1, final_g, yg):
    t = h1.shape[0]
    tc = TC_COMBINE
    nblk = t // tc

    def pair_rows(kk):
        return pl.BlockSpec((tc * ROW_TILES, LANES), lambda i: (kk * nblk + i, 0))

    return pl.pallas_call(
        _combine_kernel,
        grid=(nblk,),
        in_specs=[pl.BlockSpec((tc, LANES), lambda i: (i, 0)),
                  pl.BlockSpec((tc, D_MODEL), lambda i: (i, 0)),
                  pl.BlockSpec((1, D_MODEL), lambda i: (0, 0))] + [pair_rows(kk) for kk in range(TOP_K)],
        out_specs=pl.BlockSpec((tc, D_MODEL), lambda i: (i, 0)),
        out_shape=jax.ShapeDtypeStruct((t, D_MODEL), F32),
        compiler_params=pltpu.CompilerParams(
            dimension_semantics=("parallel",), vmem_limit_bytes=VMEM_LIMIT),
        name="combine",
    )(gates, h1, final_g, yg, yg, yg, yg)


def _hi_lo(w):
    hi = w.astype(BF16)
    return hi, (w - hi.astype(F32)).astype(BF16)


def _layer(x2, batch, seq, norm1_g, w_in, fox_f_bias, gla_gate_up, gla_gate_bias, gla_norm_g, w_out,
           norm2_g, router_w, router_b, exp_w_in, exp_b_in, exp_w_out, exp_b_out, final_g):
    t = batch * seq
    o = 0
    segs = {}
    for name, width in (("fq", FOX_WIDTH), ("fk", FOX_WIDTH), ("fv", FOX_WIDTH), ("ff", FOX_HEADS),
                        ("gq", GLA_KEY_WIDTH), ("gk", GLA_KEY_WIDTH), ("gv", GLA_VAL_WIDTH),
                        ("gl", GLA_GATE_RANK), ("go", GLA_VAL_WIDTH)):
        segs[name] = w_in[:, o:o + width]
        o += width
    w_main = jnp.concatenate(
        [segs[n] for n in ("fq", "fk", "fv", "gq", "gk", "gv", "go")], axis=1).astype(BF16)
    n_small = FOX_HEADS + GLA_GATE_RANK
    w_small = jnp.concatenate(
        [segs["ff"], segs["gl"], segs["ff"], segs["ff"],
         jnp.zeros((D_MODEL, LANES - n_small - 2 * FOX_HEADS), F32)], axis=1)
    ws_hi, ws_lo = _hi_lo(w_small)
    fb_pad = jnp.concatenate(
        [fox_f_bias, jnp.zeros((GLA_GATE_RANK,), F32), fox_f_bias, fox_f_bias,
         jnp.zeros((LANES - n_small - 2 * FOX_HEADS,), F32)]).reshape(1, LANES)
    gu_pad = jnp.pad(gla_gate_up, ((FOX_HEADS, LANES - n_small), (0, 0)))
    gu_hi, gu_lo = _hi_lo(gu_pad)

    qa, ka, vt, gq, gk, gv, go, la = _in_proj(
        x2, norm1_g.reshape(1, D_MODEL), w_main, ws_hi, ws_lo, fb_pad, gu_hi, gu_lo,
        gla_gate_bias.reshape(1, GLA_KEY_WIDTH), batch, seq)

    fox = _fox(qa, ka, vt, batch, seq)
    gla = _gla(gq, gk, la, gv, go, gla_norm_g.reshape(1, GLA_VAL_DIM), batch, seq)

    w_out_b = w_out.astype(BF16)
    rw_hi, rw_lo = _hi_lo(jnp.pad(router_w, ((0, 0), (0, LANES - N_EXPERTS))))
    rb_pad = jnp.pad(router_b, (0, LANES - N_EXPERTS), constant_values=NEG_BIG).reshape(1, LANES)
    h1, n2_tiles, ti, gates, rank, counts = _out_proj(
        x2, fox, gla, w_out_b[:FOX_WIDTH], w_out_b[FOX_WIDTH:], norm2_g.reshape(1, D_MODEL),
        rw_hi, rw_lo, rb_pad)

    cnt = counts[0, :N_EXPERTS]
    padded = ((cnt + TM_EXP - 1) // TM_EXP) * TM_EXP
    ends = jnp.cumsum(padded)
    starts = ends - padded
    expert_ids = jnp.arange(N_EXPERTS, dtype=jnp.int32)
    start_of = jnp.sum(jnp.where(ti[:, :TOP_K, None] == expert_ids, starts, 0), axis=-1)
    pos_km = (start_of + rank[:, :TOP_K]).T.reshape(-1).astype(jnp.int32)
    n_tiles = (t * TOP_K) // TM_EXP + N_EXPERTS
    num_used = (ends[-1] // TM_EXP).astype(jnp.int32)
    tile_start = jnp.arange(n_tiles, dtype=jnp.int32) * TM_EXP
    tile_expert = jnp.minimum(
        jnp.sum((tile_start[:, None] >= ends[None, :]).astype(jnp.int32), axis=1), N_EXPERTS - 1)
    last_expert = tile_expert[jnp.maximum(num_used - 1, 0)]
    tile_expert = jnp.where(jnp.arange(n_tiles) < num_used, tile_expert, last_expert).astype(jnp.int32)

    src = _slot_sources(starts.astype(jnp.int32), cnt.astype(jnp.int32), ends.astype(jnp.int32), pos_km,
                        n_tiles * TM_EXP)
    yg = _experts(tile_expert, num_used.reshape(1), src, n2_tiles, exp_w_in,
                  exp_b_in.reshape(N_EXPERTS, 1, 2 * D_FF), exp_w_out,
                  exp_b_out.reshape(N_EXPERTS, 1, D_MODEL))
    return _combine(gates, h1, final_g.reshape(1, D_MODEL), yg)


def kernel(x, norm1_g, w_in, fox_f_bias, gla_gate_up, gla_gate_bias, gla_norm_g, w_out, norm2_g,
           router_w, router_b, exp_w_in, exp_b_in, exp_w_out, exp_b_out, final_g):
    batch, seq, d = x.shape
    depth = norm1_g.shape[0]
    assert depth == 1 and d == D_MODEL
    out = _layer(x.reshape(batch * seq, d), batch, seq, norm1_g[0], w_in[0], fox_f_bias[0],
                 gla_gate_up[0], gla_gate_bias[0], gla_norm_g[0], w_out[0], norm2_g[0], router_w[0],
                 router_b[0], exp_w_in[0], exp_b_in[0], exp_w_out[0], exp_b_out[0], final_g)
    return out.reshape(batch, seq, d)
```

```python
import jax
import jax.numpy as jnp
import numpy as np
from jax import lax
from jax.experimental import pallas as pl
from jax.experimental.pallas import tpu as pltpu

F32 = jnp.float32
BF16 = jnp.bfloat16

D_MODEL = 1024
FOX_HEADS = 8
FOX_HEAD_DIM = 64
FOX_WIDTH = 512
GLA_HEADS = 4
GLA_KEY_DIM = 64
GLA_KEY_WIDTH = 256
GLA_VAL_DIM = 128
GLA_VAL_WIDTH = 512
GLA_GATE_RANK = 16
GLA_GATE_TAU = 16.0
CHUNK = 64
N_EXPERTS = 32
TOP_K = 4
D_FF = 1024
SWIGLU_LIMIT = 7.0
SWIGLU_ALPHA = 1.702
EPS = 1e-5

LANES = 128
SUBLANES = 8
ROW_TILES = D_MODEL // LANES
VMEM_LIMIT = 56 * 1024 * 1024

TM_PROJ = 256
TQ = 256
FOX_HEADS_PER_STEP = 4
GLA_ROWS = 512
TM_EXP = 256
SLOT_BLOCK = 8192
SCALAR_UNROLL = 8
TC_COMBINE = 256
QK_WIDTH = FOX_HEADS * LANES
F_COPIES = (0, FOX_HEADS + GLA_GATE_RANK, 2 * FOX_HEADS + GLA_GATE_RANK)
F_COPY2, F_COPY3 = F_COPIES[1:]
VT_ROWS = FOX_HEAD_DIM + 16
NEG_BIG = -1e30
LOG2E = 1.4426950408889634


def _log_sigmoid(z):
    return jnp.minimum(z, 0.0) - jnp.log1p(jnp.exp(-jnp.abs(z)))


def _split3(a):
    p1 = a.astype(BF16)
    r1 = a - p1.astype(F32)
    p2 = r1.astype(BF16)
    r2 = r1 - p2.astype(F32)
    return p1, p2, r2.astype(BF16)


def _dot(a, b):
    return jnp.dot(a, b, preferred_element_type=F32)


def _dot_nt(a, b):
    return lax.dot_general(a, b, (((1,), (1,)), ((), ())), preferred_element_type=F32)


def _dot_tn(a, b):
    return lax.dot_general(a, b, (((0,), (0,)), ((), ())), preferred_element_type=F32)


def _dot3_exact_lhs(tri, a):
    p1, p2, p3 = _split3(a)
    return _dot(tri, p1) + _dot(tri, p2) + _dot(tri, p3)


def _dot_hi(a, b_hi, b_lo):
    a_hi = a.astype(BF16)
    a_lo = (a - a_hi.astype(F32)).astype(BF16)
    return _dot(a_hi, b_hi) + _dot(a_lo, b_hi) + _dot(a_hi, b_lo)


def _rms(x, g):
    return x * lax.rsqrt(jnp.mean(x * x, axis=-1, keepdims=True) + EPS) * g


def _in_proj_kernel(x_ref, g_ref, wm_ref, wsh_ref, wsl_ref, fb_ref, guh_ref, gul_ref, gb_ref,
                    selq_ref, selk_ref, oneq_ref, onek_ref,
                    qa_ref, ka_ref, vt_ref, gq_ref, gk_ref, gv_ref, go_ref, la_ref, carry_ref):
    tm = x_ref.shape[0]

    @pl.when(pl.program_id(1) == 0)
    def _():
        carry_ref[...] = jnp.zeros_like(carry_ref)

    n = _rms(x_ref[...], g_ref[...])
    main = _dot(n.astype(BF16), wm_ref[...])

    small = _dot_hi(n, wsh_ref[...], wsl_ref[...])

    ls = _log_sigmoid(small + fb_ref[...])
    row = lax.broadcasted_iota(jnp.int32, (tm, tm), 0)
    col = lax.broadcasted_iota(jnp.int32, (tm, tm), 1)
    tri = jnp.where(row >= col, 1.0, 0.0).astype(BF16)
    cum = _dot3_exact_lhs(tri, ls) + carry_ref[...]
    carry_ref[...] = cum[tm - 1:tm, :]

    lane = lax.broadcasted_iota(jnp.int32, (tm, LANES), 1)
    p1, p2, p3 = _split3(cum * LOG2E)
    zero = jnp.zeros_like(p1)
    comb = jnp.where(lane < FOX_HEADS, p1,
                     jnp.where((lane >= F_COPY2) & (lane < F_COPY2 + FOX_HEADS), p2,
                               jnp.where((lane >= F_COPY3) & (lane < F_COPY3 + FOX_HEADS), p3, zero)))
    aug_q = _dot(comb, selq_ref[...]) + oneq_ref[...]
    aug_k = _dot(comb, selk_ref[...]) + onek_ref[...]
    lane1 = lax.broadcasted_iota(jnp.int32, (1, LANES), 1)
    q_scale = jnp.where(lane1 < FOX_HEAD_DIM, LOG2E * FOX_HEAD_DIM ** -0.5, 0.0)
    k_scale = jnp.where(lane1 < FOX_HEAD_DIM, 1.0, 0.0)
    ones_row = jnp.where(lax.broadcasted_iota(jnp.int32, (VT_ROWS - FOX_HEAD_DIM, tm), 0) == 0, 1.0, 0.0)
    for p in range(FOX_HEADS // 2):
        qc = main[:, p * LANES:(p + 1) * LANES]
        kc = main[:, FOX_WIDTH + p * LANES:FOX_WIDTH + (p + 1) * LANES]
        vc_t = main[:, 2 * FOX_WIDTH + p * LANES:2 * FOX_WIDTH + (p + 1) * LANES].T
        for hh in range(2):
            h = 2 * p + hh
            hs = slice(h * LANES, (h + 1) * LANES)
            if hh:
                qc = pltpu.roll(qc, FOX_HEAD_DIM, 1)
                kc = pltpu.roll(kc, FOX_HEAD_DIM, 1)
            qa_ref[:, hs] = (qc * q_scale + aug_q[:, hs]).astype(BF16)
            ka_ref[:, hs] = (kc * k_scale + aug_k[:, hs]).astype(BF16)
            vt_ref[h] = jnp.concatenate(
                [vc_t[hh * FOX_HEAD_DIM:(hh + 1) * FOX_HEAD_DIM], ones_row], axis=0).astype(BF16)
    o = 3 * FOX_WIDTH
    gq_ref[...] = (main[:, o:o + GLA_KEY_WIDTH] * (GLA_KEY_DIM ** -0.5)).astype(BF16); o += GLA_KEY_WIDTH
    gk_ref[...] = main[:, o:o + GLA_KEY_WIDTH]; o += GLA_KEY_WIDTH
    gv_ref[...] = main[:, o:o + GLA_VAL_WIDTH].astype(BF16); o += GLA_VAL_WIDTH
    go_ref[...] = main[:, o:o + GLA_VAL_WIDTH]

    z = _dot_hi(small, guh_ref[...], gul_ref[...]) + gb_ref[...]
    la_ref[...] = _log_sigmoid(z) * (1.0 / GLA_GATE_TAU)


def _aug_constants():
    selq = np.zeros((LANES, QK_WIDTH), np.float32)
    selk = np.zeros((LANES, QK_WIDTH), np.float32)
    oneq = np.zeros((1, QK_WIDTH), np.float32)
    onek = np.zeros((1, QK_WIDTH), np.float32)
    for h in range(FOX_HEADS):
        base = h * LANES + FOX_HEAD_DIM
        for i, copy in enumerate(F_COPIES):
            selq[copy + h, base + i] = 1.0
            onek[0, base + i] = 1.0
            selk[copy + h, base + 3 + i] = -1.0
            oneq[0, base + 3 + i] = 1.0
    return (jnp.asarray(selq, BF16), jnp.asarray(selk, BF16), jnp.asarray(oneq), jnp.asarray(onek))


def _in_proj(x2, norm_g, w_main, ws_hi, ws_lo, fb_pad, gu_hi, gu_lo, gb, batch, seq):
    t = batch * seq
    tm = TM_PROJ
    nj = seq // tm
    pairs = FOX_HEADS // 2
    selq, selk, oneq, onek = _aug_constants()
    rows = lambda w: pl.BlockSpec((tm, w), lambda b, j: (b * nj + j, 0))
    full = lambda a: pl.BlockSpec(a.shape, lambda b, j: (0,) * a.ndim)
    out_shape = [
        jax.ShapeDtypeStruct((t, QK_WIDTH), BF16),
        jax.ShapeDtypeStruct((t, QK_WIDTH), BF16),
        jax.ShapeDtypeStruct((FOX_HEADS, VT_ROWS, t), BF16),
        jax.ShapeDtypeStruct((t, GLA_KEY_WIDTH), BF16),
        jax.ShapeDtypeStruct((t, GLA_KEY_WIDTH), F32),
        jax.ShapeDtypeStruct((t, GLA_VAL_WIDTH), BF16),
        jax.ShapeDtypeStruct((t, GLA_VAL_WIDTH), F32),
        jax.ShapeDtypeStruct((t, GLA_KEY_WIDTH), F32),
    ]
    out_specs = [rows(QK_WIDTH), rows(QK_WIDTH),
                 pl.BlockSpec((FOX_HEADS, VT_ROWS, tm), lambda b, j: (0, 0, b * nj + j)),
                 rows(GLA_KEY_WIDTH), rows(GLA_KEY_WIDTH), rows(GLA_VAL_WIDTH), rows(GLA_VAL_WIDTH),
                 rows(GLA_KEY_WIDTH)]
    args = (x2, norm_g, w_main, ws_hi, ws_lo, fb_pad, gu_hi, gu_lo, gb, selq, selk, oneq, onek)
    return pl.pallas_call(
        _in_proj_kernel,
        grid=(batch, nj),
        in_specs=[rows(D_MODEL)] + [full(a) for a in args[1:]],
        out_specs=out_specs,
        out_shape=out_shape,
        scratch_shapes=[pltpu.VMEM((1, LANES), F32)],
        compiler_params=pltpu.CompilerParams(
            dimension_semantics=("arbitrary", "arbitrary"), vmem_limit_bytes=VMEM_LIMIT),
        name="in_proj",
    )(*args)


def _fox_kernel(q_ref, k_ref, vt_ref, o_ref, acc_ref, m_ref, sa_ref, ma_ref, sb_ref, mb_ref):
    tq = q_ref.shape[0]
    tk = tq
    nh = q_ref.shape[1] // LANES
    i = pl.program_id(2)

    m_ref[...] = jnp.full_like(m_ref, -jnp.inf)
    acc_ref[...] = jnp.zeros_like(acc_ref)
    slots = ((sa_ref, ma_ref), (sb_ref, mb_ref))

    def stage_scores(block, slot, masked=False):
        start = pl.multiple_of(block * tk, tk)
        for h in range(nh):
            s = _dot_nt(k_ref[pl.ds(start, tk), h * LANES:(h + 1) * LANES],
                        q_ref[:, h * LANES:(h + 1) * LANES])
            if masked:
                key = lax.broadcasted_iota(jnp.int32, (tk, tq), 0)
                qry = lax.broadcasted_iota(jnp.int32, (tk, tq), 1)
                s = jnp.where(key <= qry, s, -jnp.inf)
            slots[slot][0][h] = s
            slots[slot][1][h] = jnp.max(s.reshape(tk // SUBLANES, SUBLANES, tq), axis=0)

    def stage_accumulate(block, slot):
        start = pl.multiple_of(block * tk, tk)
        probs, alphas = [], []
        for h in range(nh):
            m_prev = m_ref[h]
            m_new = jnp.maximum(m_prev, jnp.max(slots[slot][1][h], axis=0, keepdims=True))
            probs.append(jnp.exp2(slots[slot][0][h] - m_new[0:1]).astype(BF16))
            alphas.append(jnp.exp2(m_prev - m_new)[0:1])
            m_ref[h] = m_new
        for h in range(nh):
            acc_ref[h] = alphas[h] * acc_ref[h] + _dot(vt_ref[h, :, pl.ds(start, tk)], probs[h])

    def processed(m):
        return jnp.where(m == 0, i, m - 1)

    stage_scores(i, 0, masked=True)
    pairs = i // 2

    def body(mm, c):
        m = 2 * mm
        stage_scores(m, 1)
        stage_accumulate(processed(m), 0)
        stage_scores(m + 1, 0)
        stage_accumulate(m, 1)
        return c

    lax.fori_loop(0, pairs, body, 0)
    last = 2 * pairs

    @pl.when(i > last)
    def _():
        stage_scores(last, 1)
        stage_accumulate(processed(last), 0)
        stage_accumulate(last, 1)

    @pl.when(i == last)
    def _():
        stage_accumulate(processed(last), 0)

    o_t = jnp.concatenate(
        [acc_ref[h, 0:FOX_HEAD_DIM, :] / acc_ref[h, FOX_HEAD_DIM:FOX_HEAD_DIM + 1, :]
         for h in range(nh)], axis=0)
    o_ref[...] = o_t.T.astype(o_ref.dtype)


def _fox(qa, ka, vt, batch, seq):
    t = batch * seq
    nq = seq // TQ
    nh = FOX_HEADS_PER_STEP
    groups = FOX_HEADS // nh
    return pl.pallas_call(
        _fox_kernel,
        grid=(batch, groups, nq),
        in_specs=[
            pl.BlockSpec((TQ, nh * LANES), lambda b, g, i: (b * nq + i, g)),
            pl.BlockSpec((seq, nh * LANES), lambda b, g, i: (b, g)),
            pl.BlockSpec((nh, VT_ROWS, seq), lambda b, g, i: (g, 0, b)),
        ],
        out_specs=pl.BlockSpec((TQ, nh * FOX_HEAD_DIM), lambda b, g, i: (b * nq + i, g)),
        out_shape=jax.ShapeDtypeStruct((t, FOX_WIDTH), BF16),
        scratch_shapes=[pltpu.VMEM((nh, VT_ROWS, TQ), F32),
                        pltpu.VMEM((nh, SUBLANES, TQ), F32),
                        pltpu.VMEM((nh, TQ, TQ), F32),
                        pltpu.VMEM((nh, SUBLANES, TQ), F32),
                        pltpu.VMEM((nh, TQ, TQ), F32),
                        pltpu.VMEM((nh, SUBLANES, TQ), F32)],
        compiler_params=pltpu.CompilerParams(
            dimension_semantics=("parallel", "parallel", "arbitrary"), vmem_limit_bytes=VMEM_LIMIT),
        name="fox_attention",
    )(qa, ka, vt)


def _gla_kernel(q_ref, k_ref, la_ref, v_ref, go_ref, ng_ref, o_ref, st_ref, obuf_ref):
    rows = q_ref.shape[0]

    @pl.when(pl.program_id(2) == 0)
    def _():
        st_ref[...] = jnp.zeros_like(st_ref)

    r = lax.broadcasted_iota(jnp.int32, (CHUNK, CHUNK), 0)
    c = lax.broadcasted_iota(jnp.int32, (CHUNK, CHUNK), 1)
    tri = jnp.where(r >= c, 1.0, 0.0).astype(BF16)
    vrow = lax.broadcasted_iota(jnp.int32, (2 * GLA_VAL_DIM, 2 * GLA_KEY_DIM), 0)
    kcol = lax.broadcasted_iota(jnp.int32, (2 * GLA_VAL_DIM, 2 * GLA_KEY_DIM), 1)
    same_head = (vrow >= GLA_VAL_DIM) == (kcol >= GLA_KEY_DIM)

    for ch in range(rows // CHUNK):
        sl = slice(ch * CHUNK, (ch + 1) * CHUNK)
        cum = _dot3_exact_lhs(tri, la_ref[sl, :])
        tot = cum[CHUNK - 1:CHUNK, :]
        kd = (k_ref[sl, :] * jnp.exp(tot - cum)).astype(BF16)
        upd_t = _dot_tn(v_ref[sl, :], kd)
        st = st_ref[...] * jnp.exp(tot) + jnp.where(same_head, upd_t, 0.0)
        st_ref[...] = st
        obuf_ref[sl, :] = _dot_nt(q_ref[sl, :], st.astype(BF16))

    o = obuf_ref[...]
    go = go_ref[...]
    ng = ng_ref[...]
    halves = []
    for h in range(2):
        hs = slice(h * GLA_VAL_DIM, (h + 1) * GLA_VAL_DIM)
        oh = o[:, hs]
        oh = oh * lax.rsqrt(jnp.mean(oh * oh, axis=-1, keepdims=True) + EPS) * ng
        gh = go[:, hs]
        halves.append(oh * (gh * jax.nn.sigmoid(gh)))
    o_ref[...] = jnp.concatenate(halves, axis=1).astype(o_ref.dtype)


def _gla(gq, gk, la, gv, go, ng, batch, seq):
    t = batch * seq
    nr = seq // GLA_ROWS
    pairs = GLA_HEADS // 2
    kspec = pl.BlockSpec((GLA_ROWS, 2 * GLA_KEY_DIM), lambda b, p, i: (b * nr + i, p))
    vspec = pl.BlockSpec((GLA_ROWS, 2 * GLA_VAL_DIM), lambda b, p, i: (b * nr + i, p))
    return pl.pallas_call(
        _gla_kernel,
        grid=(batch, pairs, nr),
        in_specs=[kspec, kspec, kspec, vspec, vspec,
                  pl.BlockSpec((1, GLA_VAL_DIM), lambda b, p, i: (0, 0))],
        out_specs=vspec,
        out_shape=jax.ShapeDtypeStruct((t, GLA_VAL_WIDTH), BF16),
        scratch_shapes=[pltpu.VMEM((2 * GLA_VAL_DIM, 2 * GLA_KEY_DIM), F32),
                        pltpu.VMEM((GLA_ROWS, 2 * GLA_VAL_DIM), F32)],
        compiler_params=pltpu.CompilerParams(
            dimension_semantics=("parallel", "parallel", "arbitrary"), vmem_limit_bytes=VMEM_LIMIT),
        name="gla",
    )(gq, gk, la, gv, go, ng)


def _out_proj_kernel(x_ref, fox_ref, gla_ref, wf_ref, wg_ref, g2_ref, rwh_ref, rwl_ref, rb_ref,
                     h_ref, n2_ref, ti_ref, gate_ref, rank_ref, cnt_ref, carry_ref):
    tm = x_ref.shape[0]

    @pl.when(pl.program_id(0) == 0)
    def _():
        carry_ref[...] = jnp.zeros_like(carry_ref)

    h = x_ref[...] + _dot(fox_ref[...], wf_ref[...]) + _dot(gla_ref[...], wg_ref[...])
    h_ref[...] = h
    n2 = _rms(h, g2_ref[...])
    for c in range(ROW_TILES):
        n2_ref[pl.ds(c, tm, stride=ROW_TILES), :] = n2[:, c * LANES:(c + 1) * LANES]

    logits = _dot_hi(n2, rwh_ref[...], rwl_ref[...]) + rb_ref[...]
    work = logits.T[0:N_EXPERTS, :]
    eidx = lax.broadcasted_iota(jnp.int32, (N_EXPERTS, tm), 0)
    vals, idxs, onehots = [], [], []
    for _ in range(TOP_K):
        m = jnp.max(work, axis=0, keepdims=True)
        idx = jnp.min(jnp.where(work == m, eidx, N_EXPERTS), axis=0, keepdims=True)
        oh = eidx == idx
        vals.append(m)
        idxs.append(idx)
        onehots.append(oh)
        work = jnp.where(oh, -jnp.inf, work)
    exps = [jnp.exp(v - vals[0]) for v in vals]
    denom = exps[0] + exps[1] + exps[2] + exps[3]

    chosen = jnp.zeros((N_EXPERTS, tm), F32)
    for oh in onehots:
        chosen = chosen + jnp.where(oh, 1.0, 0.0)
    chosen = chosen.astype(BF16)
    row = lax.broadcasted_iota(jnp.int32, (tm, tm), 0)
    col = lax.broadcasted_iota(jnp.int32, (tm, tm), 1)
    earlier = jnp.where(row < col, 1.0, 0.0).astype(BF16)
    carry = carry_ref[...]
    before = _dot(chosen, earlier) + carry
    carry = carry + _dot(chosen, jnp.ones((tm, tm), BF16))
    carry_ref[...] = carry
    cnt_ref[...] = carry[:, 0:LANES].astype(jnp.int32)

    sub = lax.broadcasted_iota(jnp.int32, (SUBLANES, tm), 0)
    ti = jnp.zeros((SUBLANES, tm), jnp.int32)
    gates = jnp.zeros((SUBLANES, tm), F32)
    ranks = jnp.zeros((SUBLANES, tm), F32)
    for kk in range(TOP_K):
        sel = sub == kk
        ti = jnp.where(sel, idxs[kk], ti)
        gates = jnp.where(sel, exps[kk] / denom, gates)
        rk = jnp.sum(jnp.where(onehots[kk], before, 0.0), axis=0, keepdims=True)
        ranks = jnp.where(sel, rk, ranks)
    ti_ref[...] = ti
    gate_ref[...] = gates
    rank_ref[...] = ranks.astype(jnp.int32)


def _out_proj(x2, fox, gla, wf, wg, g2, rw_hi, rw_lo, rb_pad):
    t = x2.shape[0]
    tm = TM_PROJ
    rows = lambda w: pl.BlockSpec((tm, w), lambda i: (i, 0))
    full = lambda a: pl.BlockSpec(a.shape, lambda i: (0,) * a.ndim)
    out_shape = [
        jax.ShapeDtypeStruct((t, D_MODEL), F32),
        jax.ShapeDtypeStruct((t * ROW_TILES, LANES), F32),
        jax.ShapeDtypeStruct((SUBLANES, t), jnp.int32),
        jax.ShapeDtypeStruct((SUBLANES, t), F32),
        jax.ShapeDtypeStruct((SUBLANES, t), jnp.int32),
        jax.ShapeDtypeStruct((N_EXPERTS, LANES), jnp.int32),
    ]
    per_token = pl.BlockSpec((SUBLANES, tm), lambda i: (0, i))
    out_specs = [rows(D_MODEL), pl.BlockSpec((tm * ROW_TILES, LANES), lambda i: (i, 0)),
                 per_token, per_token, per_token,
                 pl.BlockSpec((N_EXPERTS, LANES), lambda i: (0, 0))]
    return pl.pallas_call(
        _out_proj_kernel,
        grid=(t // tm,),
        in_specs=[rows(D_MODEL), rows(FOX_WIDTH), rows(GLA_VAL_WIDTH), full(wf), full(wg), full(g2),
                  full(rw_hi), full(rw_lo), full(rb_pad)],
        out_specs=out_specs,
        out_shape=out_shape,
        scratch_shapes=[pltpu.VMEM((N_EXPERTS, tm), F32)],
        compiler_params=pltpu.CompilerParams(
            dimension_semantics=("arbitrary",), vmem_limit_bytes=VMEM_LIMIT),
        name="out_proj_router",
    )(x2, fox, gla, wf, wg, g2, rw_hi, rw_lo, rb_pad)


def _slot_sources_kernel(starts_ref, cnt_ref, ends_ref, pos_ref, src_ref):
    g = pl.program_id(0)
    nblk = pos_ref.shape[0]
    n_slots = src_ref.shape[0]
    n_pairs = nblk * pl.num_programs(0)

    @pl.when(g == 0)
    def _():
        def fill(lo, hi, dump):
            def body(r, d):
                src_ref[r] = d
                return d + 1
            return lax.fori_loop(lo, hi, body, dump)

        def per_expert(e, dump):
            return fill(starts_ref[e] + cnt_ref[e], ends_ref[e], dump)

        dump = lax.fori_loop(0, N_EXPERTS, per_expert, jnp.int32(n_pairs))
        fill(ends_ref[N_EXPERTS - 1], n_slots, dump)

    base = g * nblk

    def body(j, c):
        for u in range(SCALAR_UNROLL):
            p = j * SCALAR_UNROLL + u
            src_ref[pos_ref[p]] = base + p
        return c

    lax.fori_loop(0, nblk // SCALAR_UNROLL, body, 0)


def _slot_sources(starts, cnt, ends, pos_km, n_slots):
    n_pairs = pos_km.shape[0]
    blk = SLOT_BLOCK
    grid_spec = pltpu.PrefetchScalarGridSpec(
        num_scalar_prefetch=3,
        grid=(n_pairs // blk,),
        in_specs=[pl.BlockSpec((blk,), lambda g, s, c, e: (g,), memory_space=pltpu.SMEM)],
        out_specs=pl.BlockSpec((n_slots,), lambda g, s, c, e: (0,), memory_space=pltpu.SMEM),
    )
    return pl.pallas_call(
        _slot_sources_kernel,
        grid_spec=grid_spec,
        out_shape=jax.ShapeDtypeStruct((n_slots,), jnp.int32),
        compiler_params=pltpu.CompilerParams(dimension_semantics=("arbitrary",)),
        name="slot_sources",
    )(starts, cnt, ends, pos_km)


def _tile_rows(ref, row):
    return ref.at[pl.ds(pl.multiple_of(row * ROW_TILES, ROW_TILES), ROW_TILES)]


def _experts_kernel(te_ref, nu_ref, src_ref, first_ref, par_ref, nxt_ref,
                    n2_ref, wi_ref, bi_ref, wo_ref, bo_ref, yg_ref,
                    wib_ref, wob_ref, wif_ref, wof_ref, xbuf_ref, ybuf_ref, gsem, ssem, wsem):
    i = pl.program_id(0)
    n_tiles = pl.num_programs(0)
    tm = xbuf_ref.shape[1] // ROW_TILES
    n_tokens = n2_ref.shape[0] // ROW_TILES
    nu = nu_ref[0]

    def issue_gather(tile, slot):
        base = jnp.minimum(tile, n_tiles - 1) * tm
        for r in range(tm):
            token = src_ref[base + r] & (n_tokens - 1)
            pltpu.make_async_copy(
                _tile_rows(n2_ref, token), xbuf_ref.at[slot, pl.ds(r * ROW_TILES, ROW_TILES)],
                gsem.at[slot]).start(priority=r % 2)

    def wait_gather(slot):
        pltpu.make_async_copy(
            n2_ref.at[pl.ds(0, tm * ROW_TILES)], xbuf_ref.at[slot], gsem.at[slot]).wait()

    def issue_scatter(tile, slot):
        base = tile * tm
        for r in range(tm):
            pltpu.make_async_copy(
                ybuf_ref.at[slot, pl.ds(r * ROW_TILES, ROW_TILES)],
                _tile_rows(yg_ref, src_ref[base + r]), ssem.at[slot]).start(priority=r % 2)

    def wait_scatter(slot):
        pltpu.make_async_copy(
            ybuf_ref.at[slot], yg_ref.at[pl.ds(0, tm * ROW_TILES)], ssem.at[slot]).wait()

    @pl.when(i == 0)
    def _():
        issue_gather(0, 0)
        issue_gather(1, 1)

    def weight_copies(expert, slot):
        return (pltpu.make_async_copy(wi_ref.at[expert], wif_ref.at[slot], wsem.at[slot, 0]),
                pltpu.make_async_copy(wo_ref.at[expert], wof_ref.at[slot], wsem.at[slot, 1]))

    @pl.when(i == 0)
    def _():
        for c in weight_copies(te_ref[0], 0):
            c.start()

    @pl.when(first_ref[i] == 1)
    def _():
        wslot = par_ref[i]

        @pl.when(nxt_ref[i] >= 0)
        def _():
            for c in weight_copies(nxt_ref[i], 1 - wslot):
                c.start()

        for c in weight_copies(te_ref[i], wslot):
            c.wait()
        wib_ref[...] = wif_ref[wslot].astype(BF16)
        wob_ref[...] = wof_ref[wslot].astype(BF16)

    yslot = lax.rem(i, 2)

    @pl.when((i >= 2) & (i < nu))
    def _():
        wait_scatter(yslot)

    @pl.when(i < nu)
    def _():
        xslot = lax.rem(i, 3)
        wait_gather(xslot)
        issue_gather(i + 2, lax.rem(i + 2, 3))
        x = jnp.concatenate(
            [xbuf_ref[xslot, pl.ds(c, tm, stride=ROW_TILES), :] for c in range(ROW_TILES)], axis=1)
        h = _dot(x.astype(BF16), wib_ref[...]) + bi_ref[0]
        gate = jnp.minimum(h[:, :D_FF], SWIGLU_LIMIT)
        lin = jnp.clip(h[:, D_FF:], -SWIGLU_LIMIT, SWIGLU_LIMIT)
        a = (lin + 1.0) * (gate * jax.nn.sigmoid(SWIGLU_ALPHA * gate))
        y = _dot(a.astype(BF16), wob_ref[...]) + bo_ref[0]
        for c in range(ROW_TILES):
            ybuf_ref[yslot, pl.ds(c, tm, stride=ROW_TILES), :] = y[:, c * LANES:(c + 1) * LANES]
        issue_scatter(i, yslot)

    @pl.when(i == nu - 1)
    def _():
        wait_gather(lax.rem(i + 1, 3))
        wait_gather(lax.rem(i + 2, 3))
        wait_scatter(yslot)

        @pl.when(i >= 1)
        def _():
            wait_scatter(1 - yslot)

    @pl.when(i >= nu)
    def _():
        ybuf_ref[0] = jnp.zeros_like(ybuf_ref[0])
        tail = pltpu.make_async_copy(
            ybuf_ref.at[0], yg_ref.at[pl.ds(pl.multiple_of(src_ref[i * tm] * ROW_TILES, ROW_TILES),
                                            tm * ROW_TILES)], ssem.at[0])
        tail.start()
        tail.wait()


def _experts(tile_expert, num_used, src, group_first, group_parity, next_expert,
             n2_tiles, w_in, b_in, w_out, b_out):
    n_tiles = tile_expert.shape[0]
    tm = TM_EXP
    bias = lambda w: pl.BlockSpec((1, 1, w), lambda i, te, *_: (te[i], 0, 0))
    grid_spec = pltpu.PrefetchScalarGridSpec(
        num_scalar_prefetch=6,
        grid=(n_tiles,),
        in_specs=[pl.BlockSpec(memory_space=pl.ANY), pl.BlockSpec(memory_space=pl.ANY), bias(2 * D_FF),
                  pl.BlockSpec(memory_space=pl.ANY), bias(D_MODEL)],
        out_specs=pl.BlockSpec(memory_space=pl.ANY),
        scratch_shapes=[pltpu.VMEM((D_MODEL, 2 * D_FF), BF16), pltpu.VMEM((D_FF, D_MODEL), BF16),
                        pltpu.VMEM((2, D_MODEL, 2 * D_FF), F32),
                        pltpu.VMEM((2, D_FF, D_MODEL), F32),
                        pltpu.VMEM((3, tm * ROW_TILES, LANES), F32),
                        pltpu.VMEM((2, tm * ROW_TILES, LANES), F32),
                        pltpu.SemaphoreType.DMA((3,)), pltpu.SemaphoreType.DMA((2,)),
                        pltpu.SemaphoreType.DMA((2, 2))],
    )
    return pl.pallas_call(
        _experts_kernel,
        grid_spec=grid_spec,
        out_shape=jax.ShapeDtypeStruct((n_tiles * tm * ROW_TILES, LANES), F32),
        compiler_params=pltpu.CompilerParams(
            dimension_semantics=("arbitrary",), vmem_limit_bytes=VMEM_LIMIT),
        name="experts",
    )(tile_expert, num_used, src, group_first, group_parity, next_expert,
      n2_tiles, w_in, b_in, w_out, b_out)


def _combine_kernel(gate_ref, h_ref, fg_ref, y0_ref, y1_ref, y2_ref, y3_ref, o_ref):
    tc = h_ref.shape[0]
    gates = gate_ref[...].T
    h = h_ref[...]
    for kk, y_ref in enumerate((y0_ref, y1_ref, y2_ref, y3_ref)):
        yk = jnp.concatenate(
            [y_ref[pl.ds(c, tc, stride=ROW_TILES), :] for c in range(ROW_TILES)], axis=1)
        h = h + gates[:, kk:kk + 1] * yk
    o_ref[...] = _rms(h, fg_ref[...])


def _combine(gates, h1, final_g, yg):
    t = h1.shape[0]
    tc = TC_COMBINE
    nblk = t // tc

    def pair_rows(kk):
        return pl.BlockSpec((tc * ROW_TILES, LANES), lambda i: (kk * nblk + i, 0))

    return pl.pallas_call(
        _combine_kernel,
        grid=(nblk,),
        in_specs=[pl.BlockSpec((SUBLANES, tc), lambda i: (0, i)),
                  pl.BlockSpec((tc, D_MODEL), lambda i: (i, 0)),
                  pl.BlockSpec((1, D_MODEL), lambda i: (0, 0))] + [pair_rows(kk) for kk in range(TOP_K)],
        out_specs=pl.BlockSpec((tc, D_MODEL), lambda i: (i, 0)),
        out_shape=jax.ShapeDtypeStruct((t, D_MODEL), F32),
        compiler_params=pltpu.CompilerParams(
            dimension_semantics=("parallel",), vmem_limit_bytes=VMEM_LIMIT),
        name="combine",
    )(gates, h1, final_g, yg, yg, yg, yg)


def _hi_lo(w):
    hi = w.astype(BF16)
    return hi, (w - hi.astype(F32)).astype(BF16)


def _layer(x2, batch, seq, norm1_g, w_in, fox_f_bias, gla_gate_up, gla_gate_bias, gla_norm_g, w_out,
           norm2_g, router_w, router_b, exp_w_in, exp_b_in, exp_w_out, exp_b_out, final_g):
    t = batch * seq
    o = 0
    segs = {}
    for name, width in (("fq", FOX_WIDTH), ("fk", FOX_WIDTH), ("fv", FOX_WIDTH), ("ff", FOX_HEADS),
                        ("gq", GLA_KEY_WIDTH), ("gk", GLA_KEY_WIDTH), ("gv", GLA_VAL_WIDTH),
                        ("gl", GLA_GATE_RANK), ("go", GLA_VAL_WIDTH)):
        segs[name] = w_in[:, o:o + width]
        o += width
    w_main = jnp.concatenate(
        [segs[n] for n in ("fq", "fk", "fv", "gq", "gk", "gv", "go")], axis=1).astype(BF16)
    n_small = FOX_HEADS + GLA_GATE_RANK
    w_small = jnp.concatenate(
        [segs["ff"], segs["gl"], segs["ff"], segs["ff"],
         jnp.zeros((D_MODEL, LANES - n_small - 2 * FOX_HEADS), F32)], axis=1)
    ws_hi, ws_lo = _hi_lo(w_small)
    fb_pad = jnp.concatenate(
        [fox_f_bias, jnp.zeros((GLA_GATE_RANK,), F32), fox_f_bias, fox_f_bias,
         jnp.zeros((LANES - n_small - 2 * FOX_HEADS,), F32)]).reshape(1, LANES)
    gu_pad = jnp.pad(gla_gate_up, ((FOX_HEADS, LANES - n_small), (0, 0)))
    gu_hi, gu_lo = _hi_lo(gu_pad)

    qa, ka, vt, gq, gk, gv, go, la = _in_proj(
        x2, norm1_g.reshape(1, D_MODEL), w_main, ws_hi, ws_lo, fb_pad, gu_hi, gu_lo,
        gla_gate_bias.reshape(1, GLA_KEY_WIDTH), batch, seq)

    fox = _fox(qa, ka, vt, batch, seq)
    gla = _gla(gq, gk, la, gv, go, gla_norm_g.reshape(1, GLA_VAL_DIM), batch, seq)

    w_out_b = w_out.astype(BF16)
    rw_hi, rw_lo = _hi_lo(jnp.pad(router_w, ((0, 0), (0, LANES - N_EXPERTS))))
    rb_pad = jnp.pad(router_b, (0, LANES - N_EXPERTS), constant_values=NEG_BIG).reshape(1, LANES)
    h1, n2_tiles, ti, gates, rank, counts = _out_proj(
        x2, fox, gla, w_out_b[:FOX_WIDTH], w_out_b[FOX_WIDTH:], norm2_g.reshape(1, D_MODEL),
        rw_hi, rw_lo, rb_pad)

    cnt = counts[:, 0]
    padded = ((cnt + TM_EXP - 1) // TM_EXP) * TM_EXP
    ends = jnp.cumsum(padded)
    starts = ends - padded
    expert_ids = jnp.arange(N_EXPERTS, dtype=jnp.int32)
    start_of = jnp.sum(jnp.where(ti[:TOP_K, :, None] == expert_ids, starts, 0), axis=-1)
    pos_km = (start_of + rank[:TOP_K]).reshape(-1).astype(jnp.int32)
    n_tiles = (t * TOP_K) // TM_EXP + N_EXPERTS
    num_used = (ends[-1] // TM_EXP).astype(jnp.int32)
    tile_start = jnp.arange(n_tiles, dtype=jnp.int32) * TM_EXP
    tile_expert = jnp.minimum(
        jnp.sum((tile_start[:, None] >= ends[None, :]).astype(jnp.int32), axis=1), N_EXPERTS - 1)
    last_expert = tile_expert[jnp.maximum(num_used - 1, 0)]
    tile_expert = jnp.where(jnp.arange(n_tiles) < num_used, tile_expert, last_expert).astype(jnp.int32)

    src = _slot_sources(starts.astype(jnp.int32), cnt.astype(jnp.int32), ends.astype(jnp.int32), pos_km,
                        n_tiles * TM_EXP)
    tile_ids = jnp.arange(n_tiles, dtype=jnp.int32)
    group_first = jnp.concatenate(
        [jnp.ones((1,), jnp.int32), (tile_expert[1:] != tile_expert[:-1]).astype(jnp.int32)])
    group_parity = (jnp.cumsum(group_first) - 1) & 1
    first_pos = jnp.where(group_first == 1, tile_ids, n_tiles)
    next_first = jnp.min(
        jnp.where(first_pos[None, :] > tile_ids[:, None], first_pos[None, :], n_tiles), axis=1)
    next_expert = jnp.where(next_first < n_tiles,
                            tile_expert[jnp.minimum(next_first, n_tiles - 1)], -1).astype(jnp.int32)
    yg = _experts(tile_expert, num_used.reshape(1), src, group_first, group_parity.astype(jnp.int32),
                  next_expert, n2_tiles, exp_w_in,
                  exp_b_in.reshape(N_EXPERTS, 1, 2 * D_FF), exp_w_out,
                  exp_b_out.reshape(N_EXPERTS, 1, D_MODEL))
    return _combine(gates, h1, final_g.reshape(1, D_MODEL), yg)


def kernel(x, norm1_g, w_in, fox_f_bias, gla_gate_up, gla_gate_bias, gla_norm_g, w_out, norm2_g,
           router_w, router_b, exp_w_in, exp_b_in, exp_w_out, exp_b_out, final_g):
    batch, seq, d = x.shape
    depth = norm1_g.shape[0]
    assert depth == 1 and d == D_MODEL
    out = _layer(x.reshape(batch * seq, d), batch, seq, norm1_g[0], w_in[0], fox_f_bias[0],
                 gla_gate_up[0], gla_gate_bias[0], gla_norm_g[0], w_out[0], norm2_g[0], router_w[0],
                 router_b[0], exp_w_in[0], exp_b_in[0], exp_w_out[0], exp_b_out[0], final_g)
    return out.reshape(batch, seq, d)
```

```python
import jax
import jax.numpy as jnp
import numpy as np
from jax import lax
from jax.experimental import pallas as pl
from jax.experimental.pallas import tpu as pltpu

F32 = jnp.float32
BF16 = jnp.bfloat16

D_MODEL = 1024
FOX_HEADS = 8
FOX_HEAD_DIM = 64
FOX_WIDTH = 512
GLA_HEADS = 4
GLA_KEY_DIM = 64
GLA_KEY_WIDTH = 256
GLA_VAL_DIM = 128
GLA_VAL_WIDTH = 512
GLA_GATE_RANK = 16
GLA_GATE_TAU = 16.0
CHUNK = 64
N_EXPERTS = 32
TOP_K = 4
D_FF = 1024
SWIGLU_LIMIT = 7.0
SWIGLU_ALPHA = 1.702
EPS = 1e-5

LANES = 128
SUBLANES = 8
ROW_TILES = D_MODEL // LANES
VMEM_LIMIT = 56 * 1024 * 1024

TM_PROJ = 256
TQ = 256
FOX_HEADS_PER_STEP = 4
GLA_ROWS = 512
TM_EXP = 256
SLOT_BLOCK = 8192
SCALAR_UNROLL = 8
TC_COMBINE = 256
QK_WIDTH = FOX_HEADS * LANES
F_COPIES = (0, FOX_HEADS + GLA_GATE_RANK, 2 * FOX_HEADS + GLA_GATE_RANK)
F_COPY2, F_COPY3 = F_COPIES[1:]
VT_ROWS = FOX_HEAD_DIM + 16
NEG_BIG = -1e30
SKIP_MARGIN = 160.0
NORM_SLACK = 1.02
LOG2E = 1.4426950408889634


def _log_sigmoid(z):
    return jnp.minimum(z, 0.0) - jnp.log1p(jnp.exp(-jnp.abs(z)))


def _split3(a):
    p1 = a.astype(BF16)
    r1 = a - p1.astype(F32)
    p2 = r1.astype(BF16)
    r2 = r1 - p2.astype(F32)
    return p1, p2, r2.astype(BF16)


def _dot(a, b):
    return jnp.dot(a, b, preferred_element_type=F32)


def _dot_nt(a, b):
    return lax.dot_general(a, b, (((1,), (1,)), ((), ())), preferred_element_type=F32)


def _dot_tn(a, b):
    return lax.dot_general(a, b, (((0,), (0,)), ((), ())), preferred_element_type=F32)


def _dot3_exact_lhs(tri, a):
    p1, p2, p3 = _split3(a)
    return _dot(tri, p1) + _dot(tri, p2) + _dot(tri, p3)


def _dot_hi(a, b_hi, b_lo):
    a_hi = a.astype(BF16)
    a_lo = (a - a_hi.astype(F32)).astype(BF16)
    return _dot(a_hi, b_hi) + _dot(a_lo, b_hi) + _dot(a_hi, b_lo)


def _rms(x, g):
    return x * lax.rsqrt(jnp.mean(x * x, axis=-1, keepdims=True) + EPS) * g


def _in_proj_kernel(x_ref, g_ref, wm_ref, wsh_ref, wsl_ref, fb_ref, guh_ref, gul_ref, gb_ref,
                    selq_ref, selk_ref, oneq_ref, onek_ref, seln_ref,
                    qa_ref, ka_ref, vt_ref, gq_ref, gk_ref, gv_ref, go_ref, la_ref, stat_ref, carry_ref):
    tm = x_ref.shape[0]

    @pl.when(pl.program_id(1) == 0)
    def _():
        carry_ref[...] = jnp.zeros_like(carry_ref)

    n = _rms(x_ref[...], g_ref[...])
    main = _dot(n.astype(BF16), wm_ref[...])

    small = _dot_hi(n, wsh_ref[...], wsl_ref[...])

    ls = _log_sigmoid(small + fb_ref[...])
    row = lax.broadcasted_iota(jnp.int32, (tm, tm), 0)
    col = lax.broadcasted_iota(jnp.int32, (tm, tm), 1)
    tri = jnp.where(row >= col, 1.0, 0.0).astype(BF16)
    cum = _dot3_exact_lhs(tri, ls) + carry_ref[...]
    carry_ref[...] = cum[tm - 1:tm, :]

    lane = lax.broadcasted_iota(jnp.int32, (tm, LANES), 1)
    p1, p2, p3 = _split3(cum * LOG2E)
    zero = jnp.zeros_like(p1)
    comb = jnp.where(lane < FOX_HEADS, p1,
                     jnp.where((lane >= F_COPY2) & (lane < F_COPY2 + FOX_HEADS), p2,
                               jnp.where((lane >= F_COPY3) & (lane < F_COPY3 + FOX_HEADS), p3, zero)))
    aug_q = _dot(comb, selq_ref[...]) + oneq_ref[...]
    aug_k = _dot(comb, selk_ref[...]) + onek_ref[...]
    lane1 = lax.broadcasted_iota(jnp.int32, (1, LANES), 1)
    q_scale = jnp.where(lane1 < FOX_HEAD_DIM, LOG2E * FOX_HEAD_DIM ** -0.5, 0.0)
    k_scale = jnp.where(lane1 < FOX_HEAD_DIM, 1.0, 0.0)
    ones_row = jnp.where(lax.broadcasted_iota(jnp.int32, (VT_ROWS - FOX_HEAD_DIM, tm), 0) == 0, 1.0, 0.0)
    for p in range(FOX_HEADS // 2):
        qc = main[:, p * LANES:(p + 1) * LANES]
        kc = main[:, FOX_WIDTH + p * LANES:FOX_WIDTH + (p + 1) * LANES]
        vc_t = main[:, 2 * FOX_WIDTH + p * LANES:2 * FOX_WIDTH + (p + 1) * LANES].T
        for hh in range(2):
            h = 2 * p + hh
            hs = slice(h * LANES, (h + 1) * LANES)
            if hh:
                qc = pltpu.roll(qc, FOX_HEAD_DIM, 1)
                kc = pltpu.roll(kc, FOX_HEAD_DIM, 1)
            qa_ref[:, hs] = (qc * q_scale + aug_q[:, hs]).astype(BF16)
            ka_ref[:, hs] = (kc * k_scale + aug_k[:, hs]).astype(BF16)
            vt_ref[h] = jnp.concatenate(
                [vc_t[hh * FOX_HEAD_DIM:(hh + 1) * FOX_HEAD_DIM], ones_row], axis=0).astype(BF16)
    col1 = lax.broadcasted_iota(jnp.int32, (1, 2 * FOX_WIDTH), 1)
    qk = main[:, 0:2 * FOX_WIDTH] * jnp.where(col1 < FOX_WIDTH, LOG2E * FOX_HEAD_DIM ** -0.5, 1.0)
    norms2 = jnp.max(_dot((qk * qk).astype(BF16), seln_ref[...]), axis=0, keepdims=True)
    f2 = cum * LOG2E
    sub = lax.broadcasted_iota(jnp.int32, (SUBLANES, LANES), 0)
    stat_ref[...] = jnp.where(sub == 0, norms2,
                              jnp.where(sub == 1, f2[0:1, :], jnp.where(sub == 2, f2[tm - 1:tm, :], 0.0)))

    o = 3 * FOX_WIDTH
    gq_ref[...] = (main[:, o:o + GLA_KEY_WIDTH] * (GLA_KEY_DIM ** -0.5)).astype(BF16); o += GLA_KEY_WIDTH
    gk_ref[...] = main[:, o:o + GLA_KEY_WIDTH]; o += GLA_KEY_WIDTH
    gv_ref[...] = main[:, o:o + GLA_VAL_WIDTH].astype(BF16); o += GLA_VAL_WIDTH
    go_ref[...] = main[:, o:o + GLA_VAL_WIDTH]

    z = _dot_hi(small, guh_ref[...], gul_ref[...]) + gb_ref[...]
    la_ref[...] = _log_sigmoid(z) * (1.0 / GLA_GATE_TAU)


def _aug_constants():
    selq = np.zeros((LANES, QK_WIDTH), np.float32)
    selk = np.zeros((LANES, QK_WIDTH), np.float32)
    oneq = np.zeros((1, QK_WIDTH), np.float32)
    onek = np.zeros((1, QK_WIDTH), np.float32)
    for h in range(FOX_HEADS):
        base = h * LANES + FOX_HEAD_DIM
        for i, copy in enumerate(F_COPIES):
            selq[copy + h, base + i] = 1.0
            onek[0, base + i] = 1.0
            selk[copy + h, base + 3 + i] = -1.0
            oneq[0, base + 3 + i] = 1.0
    seln = np.zeros((2 * FOX_WIDTH, LANES), np.float32)
    for j in range(2 * FOX_WIDTH):
        seln[j, j // FOX_HEAD_DIM] = 1.0
    return (jnp.asarray(selq, BF16), jnp.asarray(selk, BF16), jnp.asarray(oneq), jnp.asarray(onek),
            jnp.asarray(seln, BF16))


def _in_proj(x2, norm_g, w_main, ws_hi, ws_lo, fb_pad, gu_hi, gu_lo, gb, batch, seq):
    t = batch * seq
    tm = TM_PROJ
    nj = seq // tm
    selq, selk, oneq, onek, seln = _aug_constants()
    rows = lambda w: pl.BlockSpec((tm, w), lambda b, j: (b * nj + j, 0))
    full = lambda a: pl.BlockSpec(a.shape, lambda b, j: (0,) * a.ndim)
    out_shape = [
        jax.ShapeDtypeStruct((t, QK_WIDTH), BF16),
        jax.ShapeDtypeStruct((t, QK_WIDTH), BF16),
        jax.ShapeDtypeStruct((FOX_HEADS, VT_ROWS, t), BF16),
        jax.ShapeDtypeStruct((t, GLA_KEY_WIDTH), BF16),
        jax.ShapeDtypeStruct((t, GLA_KEY_WIDTH), F32),
        jax.ShapeDtypeStruct((t, GLA_VAL_WIDTH), BF16),
        jax.ShapeDtypeStruct((t, GLA_VAL_WIDTH), F32),
        jax.ShapeDtypeStruct((t, GLA_KEY_WIDTH), F32),
        jax.ShapeDtypeStruct((batch * nj * SUBLANES, LANES), F32),
    ]
    out_specs = [rows(QK_WIDTH), rows(QK_WIDTH),
                 pl.BlockSpec((FOX_HEADS, VT_ROWS, tm), lambda b, j: (0, 0, b * nj + j)),
                 rows(GLA_KEY_WIDTH), rows(GLA_KEY_WIDTH), rows(GLA_VAL_WIDTH), rows(GLA_VAL_WIDTH),
                 rows(GLA_KEY_WIDTH),
                 pl.BlockSpec((SUBLANES, LANES), lambda b, j: (b * nj + j, 0))]
    args = (x2, norm_g, w_main, ws_hi, ws_lo, fb_pad, gu_hi, gu_lo, gb, selq, selk, oneq, onek, seln)
    return pl.pallas_call(
        _in_proj_kernel,
        grid=(batch, nj),
        in_specs=[rows(D_MODEL)] + [full(a) for a in args[1:]],
        out_specs=out_specs,
        out_shape=out_shape,
        scratch_shapes=[pltpu.VMEM((1, LANES), F32)],
        compiler_params=pltpu.CompilerParams(
            dimension_semantics=("arbitrary", "arbitrary"), vmem_limit_bytes=VMEM_LIMIT),
        name="in_proj",
    )(*args)


def _fox_kernel(lo_ref, q_ref, k_ref, vt_ref, o_ref, acc_ref, m_ref, sa_ref, ma_ref, sb_ref, mb_ref):
    tq = q_ref.shape[0]
    tk = tq
    nh = q_ref.shape[1] // LANES
    i = pl.program_id(2)

    m_ref[...] = jnp.full_like(m_ref, -jnp.inf)
    acc_ref[...] = jnp.zeros_like(acc_ref)
    slots = ((sa_ref, ma_ref), (sb_ref, mb_ref))

    def stage_scores(block, slot, masked=False):
        start = pl.multiple_of(block * tk, tk)
        for h in range(nh):
            s = _dot_nt(k_ref[pl.ds(start, tk), h * LANES:(h + 1) * LANES],
                        q_ref[:, h * LANES:(h + 1) * LANES])
            if masked:
                key = lax.broadcasted_iota(jnp.int32, (tk, tq), 0)
                qry = lax.broadcasted_iota(jnp.int32, (tk, tq), 1)
                s = jnp.where(key <= qry, s, -jnp.inf)
            slots[slot][0][h] = s
            slots[slot][1][h] = jnp.max(s.reshape(tk // SUBLANES, SUBLANES, tq), axis=0)

    def stage_accumulate(block, slot):
        start = pl.multiple_of(block * tk, tk)
        probs, alphas = [], []
        for h in range(nh):
            m_prev = m_ref[h]
            m_new = jnp.maximum(m_prev, jnp.max(slots[slot][1][h], axis=0, keepdims=True))
            probs.append(jnp.exp2(slots[slot][0][h] - m_new[0:1]).astype(BF16))
            alphas.append(jnp.exp2(m_prev - m_new)[0:1])
            m_ref[h] = m_new
        for h in range(nh):
            acc_ref[h] = alphas[h] * acc_ref[h] + _dot(vt_ref[h, :, pl.ds(start, tk)], probs[h])

    lo = lo_ref[(pl.program_id(0) * pl.num_programs(1) + pl.program_id(1)) * pl.num_programs(2) + i]
    n = i - lo

    def processed(m):
        return jnp.where(m == 0, i, lo + m - 1)

    stage_scores(i, 0, masked=True)
    pairs = n // 2

    def body(mm, c):
        m = 2 * mm
        stage_scores(lo + m, 1)
        stage_accumulate(processed(m), 0)
        stage_scores(lo + m + 1, 0)
        stage_accumulate(lo + m, 1)
        return c

    lax.fori_loop(0, pairs, body, 0)
    last = 2 * pairs

    @pl.when(n > last)
    def _():
        stage_scores(lo + last, 1)
        stage_accumulate(processed(last), 0)
        stage_accumulate(lo + last, 1)

    @pl.when(n == last)
    def _():
        stage_accumulate(processed(last), 0)

    o_t = jnp.concatenate(
        [acc_ref[h, 0:FOX_HEAD_DIM, :] / acc_ref[h, FOX_HEAD_DIM:FOX_HEAD_DIM + 1, :]
         for h in range(nh)], axis=0)
    o_ref[...] = o_t.T.astype(o_ref.dtype)


def _fox(first_block, qa, ka, vt, batch, seq):
    t = batch * seq
    nq = seq // TQ
    nh = FOX_HEADS_PER_STEP
    groups = FOX_HEADS // nh
    grid_spec = pltpu.PrefetchScalarGridSpec(
        num_scalar_prefetch=1,
        grid=(batch, groups, nq),
        in_specs=[
            pl.BlockSpec((TQ, nh * LANES), lambda b, g, i, lo: (b * nq + i, g)),
            pl.BlockSpec((seq, nh * LANES), lambda b, g, i, lo: (b, g)),
            pl.BlockSpec((nh, VT_ROWS, seq), lambda b, g, i, lo: (g, 0, b)),
        ],
        out_specs=pl.BlockSpec((TQ, nh * FOX_HEAD_DIM), lambda b, g, i, lo: (b * nq + i, g)),
        scratch_shapes=[pltpu.VMEM((nh, VT_ROWS, TQ), F32),
                        pltpu.VMEM((nh, SUBLANES, TQ), F32),
                        pltpu.VMEM((nh, TQ, TQ), F32),
                        pltpu.VMEM((nh, SUBLANES, TQ), F32),
                        pltpu.VMEM((nh, TQ, TQ), F32),
                        pltpu.VMEM((nh, SUBLANES, TQ), F32)],
    )
    return pl.pallas_call(
        _fox_kernel,
        grid_spec=grid_spec,
        out_shape=jax.ShapeDtypeStruct((t, FOX_WIDTH), BF16),
        compiler_params=pltpu.CompilerParams(
            dimension_semantics=("parallel", "parallel", "arbitrary"), vmem_limit_bytes=VMEM_LIMIT),
        name="fox_attention",
    )(first_block, qa, ka, vt)


def _gla_kernel(q_ref, k_ref, la_ref, v_ref, go_ref, ng_ref, o_ref, st_ref, obuf_ref):
    rows = q_ref.shape[0]

    @pl.when(pl.program_id(2) == 0)
    def _():
        st_ref[...] = jnp.zeros_like(st_ref)

    r = lax.broadcasted_iota(jnp.int32, (CHUNK, CHUNK), 0)
    c = lax.broadcasted_iota(jnp.int32, (CHUNK, CHUNK), 1)
    tri = jnp.where(r >= c, 1.0, 0.0).astype(BF16)
    vrow = lax.broadcasted_iota(jnp.int32, (2 * GLA_VAL_DIM, 2 * GLA_KEY_DIM), 0)
    kcol = lax.broadcasted_iota(jnp.int32, (2 * GLA_VAL_DIM, 2 * GLA_KEY_DIM), 1)
    same_head = (vrow >= GLA_VAL_DIM) == (kcol >= GLA_KEY_DIM)

    for ch in range(rows // CHUNK):
        sl = slice(ch * CHUNK, (ch + 1) * CHUNK)
        cum = _dot3_exact_lhs(tri, la_ref[sl, :])
        tot = cum[CHUNK - 1:CHUNK, :]
        kd = (k_ref[sl, :] * jnp.exp(tot - cum)).astype(BF16)
        upd_t = _dot_tn(v_ref[sl, :], kd)
        st = st_ref[...] * jnp.exp(tot) + jnp.where(same_head, upd_t, 0.0)
        st_ref[...] = st
        obuf_ref[sl, :] = _dot_nt(q_ref[sl, :], st.astype(BF16))

    o = obuf_ref[...]
    go = go_ref[...]
    ng = ng_ref[...]
    halves = []
    for h in range(2):
        hs = slice(h * GLA_VAL_DIM, (h + 1) * GLA_VAL_DIM)
        oh = o[:, hs]
        oh = oh * lax.rsqrt(jnp.mean(oh * oh, axis=-1, keepdims=True) + EPS) * ng
        gh = go[:, hs]
        halves.append(oh * (gh * jax.nn.sigmoid(gh)))
    o_ref[...] = jnp.concatenate(halves, axis=1).astype(o_ref.dtype)


def _gla(gq, gk, la, gv, go, ng, batch, seq):
    t = batch * seq
    nr = seq // GLA_ROWS
    pairs = GLA_HEADS // 2
    kspec = pl.BlockSpec((GLA_ROWS, 2 * GLA_KEY_DIM), lambda b, p, i: (b * nr + i, p))
    vspec = pl.BlockSpec((GLA_ROWS, 2 * GLA_VAL_DIM), lambda b, p, i: (b * nr + i, p))
    return pl.pallas_call(
        _gla_kernel,
        grid=(batch, pairs, nr),
        in_specs=[kspec, kspec, kspec, vspec, vspec,
                  pl.BlockSpec((1, GLA_VAL_DIM), lambda b, p, i: (0, 0))],
        out_specs=vspec,
        out_shape=jax.ShapeDtypeStruct((t, GLA_VAL_WIDTH), BF16),
        scratch_shapes=[pltpu.VMEM((2 * GLA_VAL_DIM, 2 * GLA_KEY_DIM), F32),
                        pltpu.VMEM((GLA_ROWS, 2 * GLA_VAL_DIM), F32)],
        compiler_params=pltpu.CompilerParams(
            dimension_semantics=("parallel", "parallel", "arbitrary"), vmem_limit_bytes=VMEM_LIMIT),
        name="gla",
    )(gq, gk, la, gv, go, ng)


def _out_proj_kernel(x_ref, fox_ref, gla_ref, wf_ref, wg_ref, g2_ref, rwh_ref, rwl_ref, rb_ref,
                     h_ref, n2_ref, ti_ref, gate_ref, rank_ref, cnt_ref, carry_ref):
    tm = x_ref.shape[0]

    @pl.when(pl.program_id(0) == 0)
    def _():
        carry_ref[...] = jnp.zeros_like(carry_ref)

    h = x_ref[...] + _dot(fox_ref[...], wf_ref[...]) + _dot(gla_ref[...], wg_ref[...])
    h_ref[...] = h
    n2 = _rms(h, g2_ref[...])
    for c in range(ROW_TILES):
        n2_ref[pl.ds(c, tm, stride=ROW_TILES), :] = n2[:, c * LANES:(c + 1) * LANES]

    logits = _dot_hi(n2, rwh_ref[...], rwl_ref[...]) + rb_ref[...]
    work = logits.T[0:N_EXPERTS, :]
    eidx = lax.broadcasted_iota(jnp.int32, (N_EXPERTS, tm), 0)
    vals, idxs, onehots = [], [], []
    for _ in range(TOP_K):
        m = jnp.max(work, axis=0, keepdims=True)
        idx = jnp.min(jnp.where(work == m, eidx, N_EXPERTS), axis=0, keepdims=True)
        oh = eidx == idx
        vals.append(m)
        idxs.append(idx)
        onehots.append(oh)
        work = jnp.where(oh, -jnp.inf, work)
    exps = [jnp.exp(v - vals[0]) for v in vals]
    denom = exps[0] + exps[1] + exps[2] + exps[3]

    chosen = jnp.zeros((N_EXPERTS, tm), F32)
    for oh in onehots:
        chosen = chosen + jnp.where(oh, 1.0, 0.0)
    chosen = chosen.astype(BF16)
    row = lax.broadcasted_iota(jnp.int32, (tm, tm), 0)
    col = lax.broadcasted_iota(jnp.int32, (tm, tm), 1)
    earlier = jnp.where(row < col, 1.0, 0.0).astype(BF16)
    carry = carry_ref[...]
    before = _dot(chosen, earlier) + carry
    carry = carry + _dot(chosen, jnp.ones((tm, tm), BF16))
    carry_ref[...] = carry
    cnt_ref[...] = carry[:, 0:LANES].astype(jnp.int32)

    sub = lax.broadcasted_iota(jnp.int32, (SUBLANES, tm), 0)
    ti = jnp.zeros((SUBLANES, tm), jnp.int32)
    gates = jnp.zeros((SUBLANES, tm), F32)
    ranks = jnp.zeros((SUBLANES, tm), F32)
    for kk in range(TOP_K):
        sel = sub == kk
        ti = jnp.where(sel, idxs[kk], ti)
        gates = jnp.where(sel, exps[kk] / denom, gates)
        rk = jnp.sum(jnp.where(onehots[kk], before, 0.0), axis=0, keepdims=True)
        ranks = jnp.where(sel, rk, ranks)
    ti_ref[...] = ti
    gate_ref[...] = gates
    rank_ref[...] = ranks.astype(jnp.int32)


def _out_proj(x2, fox, gla, wf, wg, g2, rw_hi, rw_lo, rb_pad):
    t = x2.shape[0]
    tm = TM_PROJ
    rows = lambda w: pl.BlockSpec((tm, w), lambda i: (i, 0))
    full = lambda a: pl.BlockSpec(a.shape, lambda i: (0,) * a.ndim)
    out_shape = [
        jax.ShapeDtypeStruct((t, D_MODEL), F32),
        jax.ShapeDtypeStruct((t * ROW_TILES, LANES), F32),
        jax.ShapeDtypeStruct((SUBLANES, t), jnp.int32),
        jax.ShapeDtypeStruct((SUBLANES, t), F32),
        jax.ShapeDtypeStruct((SUBLANES, t), jnp.int32),
        jax.ShapeDtypeStruct((N_EXPERTS, LANES), jnp.int32),
    ]
    per_token = pl.BlockSpec((SUBLANES, tm), lambda i: (0, i))
    out_specs = [rows(D_MODEL), pl.BlockSpec((tm * ROW_TILES, LANES), lambda i: (i, 0)),
                 per_token, per_token, per_token,
                 pl.BlockSpec((N_EXPERTS, LANES), lambda i: (0, 0))]
    return pl.pallas_call(
        _out_proj_kernel,
        grid=(t // tm,),
        in_specs=[rows(D_MODEL), rows(FOX_WIDTH), rows(GLA_VAL_WIDTH), full(wf), full(wg), full(g2),
                  full(rw_hi), full(rw_lo), full(rb_pad)],
        out_specs=out_specs,
        out_shape=out_shape,
        scratch_shapes=[pltpu.VMEM((N_EXPERTS, tm), F32)],
        compiler_params=pltpu.CompilerParams(
            dimension_semantics=("arbitrary",), vmem_limit_bytes=VMEM_LIMIT),
        name="out_proj_router",
    )(x2, fox, gla, wf, wg, g2, rw_hi, rw_lo, rb_pad)


def _slot_sources_kernel(starts_ref, cnt_ref, ends_ref, pos_ref, src_ref):
    g = pl.program_id(0)
    nblk = pos_ref.shape[0]
    n_slots = src_ref.shape[0]
    n_pairs = nblk * pl.num_programs(0)

    @pl.when(g == 0)
    def _():
        def fill(lo, hi, dump):
            def body(r, d):
                src_ref[r] = d
                return d + 1
            return lax.fori_loop(lo, hi, body, dump)

        def per_expert(e, dump):
            return fill(starts_ref[e] + cnt_ref[e], ends_ref[e], dump)

        dump = lax.fori_loop(0, N_EXPERTS, per_expert, jnp.int32(n_pairs))
        fill(ends_ref[N_EXPERTS - 1], n_slots, dump)

    base = g * nblk

    def body(j, c):
        for u in range(SCALAR_UNROLL):
            p = j * SCALAR_UNROLL + u
            src_ref[pos_ref[p]] = base + p
        return c

    lax.fori_loop(0, nblk // SCALAR_UNROLL, body, 0)


def _slot_sources(starts, cnt, ends, pos_km, n_slots):
    n_pairs = pos_km.shape[0]
    blk = SLOT_BLOCK
    grid_spec = pltpu.PrefetchScalarGridSpec(
        num_scalar_prefetch=3,
        grid=(n_pairs // blk,),
        in_specs=[pl.BlockSpec((blk,), lambda g, s, c, e: (g,), memory_space=pltpu.SMEM)],
        out_specs=pl.BlockSpec((n_slots,), lambda g, s, c, e: (0,), memory_space=pltpu.SMEM),
    )
    return pl.pallas_call(
        _slot_sources_kernel,
        grid_spec=grid_spec,
        out_shape=jax.ShapeDtypeStruct((n_slots,), jnp.int32),
        compiler_params=pltpu.CompilerParams(dimension_semantics=("arbitrary",)),
        name="slot_sources",
    )(starts, cnt, ends, pos_km)


def _tile_rows(ref, row):
    return ref.at[pl.ds(pl.multiple_of(row * ROW_TILES, ROW_TILES), ROW_TILES)]


def _experts_kernel(te_ref, nu_ref, src_ref, first_ref, par_ref, nxt_ref,
                    n2_ref, wi_ref, bi_ref, wo_ref, bo_ref, yg_ref,
                    wib_ref, wob_ref, wif_ref, wof_ref, xbuf_ref, ybuf_ref, gsem, ssem, wsem):
    i = pl.program_id(0)
    n_tiles = pl.num_programs(0)
    tm = xbuf_ref.shape[1] // ROW_TILES
    n_tokens = n2_ref.shape[0] // ROW_TILES
    nu = nu_ref[0]

    def issue_gather(tile, slot):
        base = jnp.minimum(tile, n_tiles - 1) * tm
        for r in range(tm):
            token = src_ref[base + r] & (n_tokens - 1)
            pltpu.make_async_copy(
                _tile_rows(n2_ref, token), xbuf_ref.at[slot, pl.ds(r * ROW_TILES, ROW_TILES)],
                gsem.at[slot]).start(priority=r % 2)

    def wait_gather(slot):
        pltpu.make_async_copy(
            n2_ref.at[pl.ds(0, tm * ROW_TILES)], xbuf_ref.at[slot], gsem.at[slot]).wait()

    def issue_scatter(tile, slot):
        base = tile * tm
        for r in range(tm):
            pltpu.make_async_copy(
                ybuf_ref.at[slot, pl.ds(r * ROW_TILES, ROW_TILES)],
                _tile_rows(yg_ref, src_ref[base + r]), ssem.at[slot]).start(priority=r % 2)

    def wait_scatter(slot):
        pltpu.make_async_copy(
            ybuf_ref.at[slot], yg_ref.at[pl.ds(0, tm * ROW_TILES)], ssem.at[slot]).wait()

    @pl.when(i == 0)
    def _():
        issue_gather(0, 0)
        issue_gather(1, 1)

    def weight_copies(expert, slot):
        return (pltpu.make_async_copy(wi_ref.at[expert], wif_ref.at[slot], wsem.at[slot, 0]),
                pltpu.make_async_copy(wo_ref.at[expert], wof_ref.at[slot], wsem.at[slot, 1]))

    @pl.when(i == 0)
    def _():
        for c in weight_copies(te_ref[0], 0):
            c.start()

    @pl.when(first_ref[i] == 1)
    def _():
        wslot = par_ref[i]

        @pl.when(nxt_ref[i] >= 0)
        def _():
            for c in weight_copies(nxt_ref[i], 1 - wslot):
                c.start()

        for c in weight_copies(te_ref[i], wslot):
            c.wait()
        wib_ref[...] = wif_ref[wslot].astype(BF16)
        wob_ref[...] = wof_ref[wslot].astype(BF16)

    yslot = lax.rem(i, 2)

    @pl.when((i >= 2) & (i < nu))
    def _():
        wait_scatter(yslot)

    @pl.when(i < nu)
    def _():
        xslot = lax.rem(i, 3)
        wait_gather(xslot)
        issue_gather(i + 2, lax.rem(i + 2, 3))
        x = jnp.concatenate(
            [xbuf_ref[xslot, pl.ds(c, tm, stride=ROW_TILES), :] for c in range(ROW_TILES)], axis=1)
        h = _dot(x.astype(BF16), wib_ref[...]) + bi_ref[0]
        gate = jnp.minimum(h[:, :D_FF], SWIGLU_LIMIT)
        lin = jnp.clip(h[:, D_FF:], -SWIGLU_LIMIT, SWIGLU_LIMIT)
        a = (lin + 1.0) * (gate * jax.nn.sigmoid(SWIGLU_ALPHA * gate))
        y = _dot(a.astype(BF16), wob_ref[...]) + bo_ref[0]
        for c in range(ROW_TILES):
            ybuf_ref[yslot, pl.ds(c, tm, stride=ROW_TILES), :] = y[:, c * LANES:(c + 1) * LANES]
        issue_scatter(i, yslot)

    @pl.when(i == nu - 1)
    def _():
        wait_gather(lax.rem(i + 1, 3))
        wait_gather(lax.rem(i + 2, 3))
        wait_scatter(yslot)

        @pl.when(i >= 1)
        def _():
            wait_scatter(1 - yslot)

    @pl.when(i >= nu)
    def _():
        ybuf_ref[0] = jnp.zeros_like(ybuf_ref[0])
        tail = pltpu.make_async_copy(
            ybuf_ref.at[0], yg_ref.at[pl.ds(pl.multiple_of(src_ref[i * tm] * ROW_TILES, ROW_TILES),
                                            tm * ROW_TILES)], ssem.at[0])
        tail.start()
        tail.wait()


def _experts(tile_expert, num_used, src, group_first, group_parity, next_expert,
             n2_tiles, w_in, b_in, w_out, b_out):
    n_tiles = tile_expert.shape[0]
    tm = TM_EXP
    bias = lambda w: pl.BlockSpec((1, 1, w), lambda i, te, *_: (te[i], 0, 0))
    grid_spec = pltpu.PrefetchScalarGridSpec(
        num_scalar_prefetch=6,
        grid=(n_tiles,),
        in_specs=[pl.BlockSpec(memory_space=pl.ANY), pl.BlockSpec(memory_space=pl.ANY), bias(2 * D_FF),
                  pl.BlockSpec(memory_space=pl.ANY), bias(D_MODEL)],
        out_specs=pl.BlockSpec(memory_space=pl.ANY),
        scratch_shapes=[pltpu.VMEM((D_MODEL, 2 * D_FF), BF16), pltpu.VMEM((D_FF, D_MODEL), BF16),
                        pltpu.VMEM((2, D_MODEL, 2 * D_FF), F32),
                        pltpu.VMEM((2, D_FF, D_MODEL), F32),
                        pltpu.VMEM((3, tm * ROW_TILES, LANES), F32),
                        pltpu.VMEM((2, tm * ROW_TILES, LANES), F32),
                        pltpu.SemaphoreType.DMA((3,)), pltpu.SemaphoreType.DMA((2,)),
                        pltpu.SemaphoreType.DMA((2, 2))],
    )
    return pl.pallas_call(
        _experts_kernel,
        grid_spec=grid_spec,
        out_shape=jax.ShapeDtypeStruct((n_tiles * tm * ROW_TILES, LANES), F32),
        compiler_params=pltpu.CompilerParams(
            dimension_semantics=("arbitrary",), vmem_limit_bytes=VMEM_LIMIT),
        name="experts",
    )(tile_expert, num_used, src, group_first, group_parity, next_expert,
      n2_tiles, w_in, b_in, w_out, b_out)


def _combine_kernel(gate_ref, h_ref, fg_ref, y0_ref, y1_ref, y2_ref, y3_ref, o_ref):
    tc = h_ref.shape[0]
    gates = gate_ref[...].T
    h = h_ref[...]
    for kk, y_ref in enumerate((y0_ref, y1_ref, y2_ref, y3_ref)):
        yk = jnp.concatenate(
            [y_ref[pl.ds(c, tc, stride=ROW_TILES), :] for c in range(ROW_TILES)], axis=1)
        h = h + gates[:, kk:kk + 1] * yk
    o_ref[...] = _rms(h, fg_ref[...])


def _combine(gates, h1, final_g, yg):
    t = h1.shape[0]
    tc = TC_COMBINE
    nblk = t // tc

    def pair_rows(kk):
        return pl.BlockSpec((tc * ROW_TILES, LANES), lambda i: (kk * nblk + i, 0))

    return pl.pallas_call(
        _combine_kernel,
        grid=(nblk,),
        in_specs=[pl.BlockSpec((SUBLANES, tc), lambda i: (0, i)),
                  pl.BlockSpec((tc, D_MODEL), lambda i: (i, 0)),
                  pl.BlockSpec((1, D_MODEL), lambda i: (0, 0))] + [pair_rows(kk) for kk in range(TOP_K)],
        out_specs=pl.BlockSpec((tc, D_MODEL), lambda i: (i, 0)),
        out_shape=jax.ShapeDtypeStruct((t, D_MODEL), F32),
        compiler_params=pltpu.CompilerParams(
            dimension_semantics=("parallel",), vmem_limit_bytes=VMEM_LIMIT),
        name="combine",
    )(gates, h1, final_g, yg, yg, yg, yg)


def _hi_lo(w):
    hi = w.astype(BF16)
    return hi, (w - hi.astype(F32)).astype(BF16)


def _first_needed_block(stats, batch, seq):
    assert TM_PROJ == TQ
    nq = seq // TQ
    st = stats.reshape(batch, nq, SUBLANES, LANES)
    q_norm = jnp.sqrt(st[:, :, 0, 0:FOX_HEADS])
    k_norm = jnp.sqrt(jnp.max(st[:, :, 0, FOX_HEADS:2 * FOX_HEADS], axis=1))
    f_first = st[:, :, 1, 0:FOX_HEADS]
    f_last = st[:, :, 2, 0:FOX_HEADS]
    bound = SKIP_MARGIN + 2.0 * NORM_SLACK * q_norm * k_norm[:, None, :]
    decay = f_first[:, :, None, :] - f_last[:, None, :, :]
    tile = jnp.arange(nq, dtype=jnp.int32)
    needed = (tile[None, :] < tile[:, None])[None, :, :, None] & (decay >= -bound[:, :, None, :])
    lo = jnp.min(jnp.where(needed, tile[None, None, :, None], tile[None, :, None, None]), axis=2)
    lo = jnp.min(lo.reshape(batch, nq, FOX_HEADS // FOX_HEADS_PER_STEP, FOX_HEADS_PER_STEP), axis=-1)
    return lo.transpose(0, 2, 1).reshape(-1).astype(jnp.int32)


def _layer(x2, batch, seq, norm1_g, w_in, fox_f_bias, gla_gate_up, gla_gate_bias, gla_norm_g, w_out,
           norm2_g, router_w, router_b, exp_w_in, exp_b_in, exp_w_out, exp_b_out, final_g):
    t = batch * seq
    o = 0
    segs = {}
    for name, width in (("fq", FOX_WIDTH), ("fk", FOX_WIDTH), ("fv", FOX_WIDTH), ("ff", FOX_HEADS),
                        ("gq", GLA_KEY_WIDTH), ("gk", GLA_KEY_WIDTH), ("gv", GLA_VAL_WIDTH),
                        ("gl", GLA_GATE_RANK), ("go", GLA_VAL_WIDTH)):
        segs[name] = w_in[:, o:o + width]
        o += width
    w_main = jnp.concatenate(
        [segs[n] for n in ("fq", "fk", "fv", "gq", "gk", "gv", "go")], axis=1).astype(BF16)
    n_small = FOX_HEADS + GLA_GATE_RANK
    w_small = jnp.concatenate(
        [segs["ff"], segs["gl"], segs["ff"], segs["ff"],
         jnp.zeros((D_MODEL, LANES - n_small - 2 * FOX_HEADS), F32)], axis=1)
    ws_hi, ws_lo = _hi_lo(w_small)
    fb_pad = jnp.concatenate(
        [fox_f_bias, jnp.zeros((GLA_GATE_RANK,), F32), fox_f_bias, fox_f_bias,
         jnp.zeros((LANES - n_small - 2 * FOX_HEADS,), F32)]).reshape(1, LANES)
    gu_pad = jnp.pad(gla_gate_up, ((FOX_HEADS, LANES - n_small), (0, 0)))
    gu_hi, gu_lo = _hi_lo(gu_pad)

    qa, ka, vt, gq, gk, gv, go, la, stats = _in_proj(
        x2, norm1_g.reshape(1, D_MODEL), w_main, ws_hi, ws_lo, fb_pad, gu_hi, gu_lo,
        gla_gate_bias.reshape(1, GLA_KEY_WIDTH), batch, seq)

    fox = _fox(_first_needed_block(stats, batch, seq), qa, ka, vt, batch, seq)
    gla = _gla(gq, gk, la, gv, go, gla_norm_g.reshape(1, GLA_VAL_DIM), batch, seq)

    w_out_b = w_out.astype(BF16)
    rw_hi, rw_lo = _hi_lo(jnp.pad(router_w, ((0, 0), (0, LANES - N_EXPERTS))))
    rb_pad = jnp.pad(router_b, (0, LANES - N_EXPERTS), constant_values=NEG_BIG).reshape(1, LANES)
    h1, n2_tiles, ti, gates, rank, counts = _out_proj(
        x2, fox, gla, w_out_b[:FOX_WIDTH], w_out_b[FOX_WIDTH:], norm2_g.reshape(1, D_MODEL),
        rw_hi, rw_lo, rb_pad)

    cnt = counts[:, 0]
    padded = ((cnt + TM_EXP - 1) // TM_EXP) * TM_EXP
    ends = jnp.cumsum(padded)
    starts = ends - padded
    expert_ids = jnp.arange(N_EXPERTS, dtype=jnp.int32)
    start_of = jnp.sum(jnp.where(ti[:TOP_K, :, None] == expert_ids, starts, 0), axis=-1)
    pos_km = (start_of + rank[:TOP_K]).reshape(-1).astype(jnp.int32)
    n_tiles = (t * TOP_K) // TM_EXP + N_EXPERTS
    num_used = (ends[-1] // TM_EXP).astype(jnp.int32)
    tile_start = jnp.arange(n_tiles, dtype=jnp.int32) * TM_EXP
    tile_expert = jnp.minimum(
        jnp.sum((tile_start[:, None] >= ends[None, :]).astype(jnp.int32), axis=1), N_EXPERTS - 1)
    last_expert = tile_expert[jnp.maximum(num_used - 1, 0)]
    tile_expert = jnp.where(jnp.arange(n_tiles) < num_used, tile_expert, last_expert).astype(jnp.int32)

    src = _slot_sources(starts.astype(jnp.int32), cnt.astype(jnp.int32), ends.astype(jnp.int32), pos_km,
                        n_tiles * TM_EXP)
    tile_ids = jnp.arange(n_tiles, dtype=jnp.int32)
    group_first = jnp.concatenate(
        [jnp.ones((1,), jnp.int32), (tile_expert[1:] != tile_expert[:-1]).astype(jnp.int32)])
    group_parity = (jnp.cumsum(group_first) - 1) & 1
    first_pos = jnp.where(group_first == 1, tile_ids, n_tiles)
    next_first = jnp.min(
        jnp.where(first_pos[None, :] > tile_ids[:, None], first_pos[None, :], n_tiles), axis=1)
    next_expert = jnp.where(next_first < n_tiles,
                            tile_expert[jnp.minimum(next_first, n_tiles - 1)], -1).astype(jnp.int32)
    yg = _experts(tile_expert, num_used.reshape(1), src, group_first, group_parity.astype(jnp.int32),
                  next_expert, n2_tiles, exp_w_in,
                  exp_b_in.reshape(N_EXPERTS, 1, 2 * D_FF), exp_w_out,
                  exp_b_out.reshape(N_EXPERTS, 1, D_MODEL))
    return _combine(gates, h1, final_g.reshape(1, D_MODEL), yg)


def kernel(x, norm1_g, w_in, fox_f_bias, gla_gate_up, gla_gate_bias, gla_norm_g, w_out, norm2_g,
           router_w, router_b, exp_w_in, exp_b_in, exp_w_out, exp_b_out, final_g):
    batch, seq, d = x.shape
    depth = norm1_g.shape[0]
    assert depth == 1 and d == D_MODEL
    out = _layer(x.reshape(batch * seq, d), batch, seq, norm1_g[0], w_in[0], fox_f_bias[0],
                 gla_gate_up[0], gla_gate_bias[0], gla_norm_g[0], w_out[0], norm2_g[0], router_w[0],
                 router_b[0], exp_w_in[0], exp_b_in[0], exp_w_out[0], exp_b_out[0], final_g)
    return out.reshape(batch, seq, d)
```

```python
import jax
import jax.numpy as jnp
import numpy as np
from jax import lax
from jax.experimental import pallas as pl
from jax.experimental.pallas import tpu as pltpu

F32 = jnp.float32
BF16 = jnp.bfloat16

D_MODEL = 1024
FOX_HEADS = 8
FOX_HEAD_DIM = 64
FOX_WIDTH = 512
GLA_HEADS = 4
GLA_KEY_DIM = 64
GLA_KEY_WIDTH = 256
GLA_VAL_DIM = 128
GLA_VAL_WIDTH = 512
GLA_GATE_RANK = 16
GLA_GATE_TAU = 16.0
CHUNK = 64
N_EXPERTS = 32
TOP_K = 4
D_FF = 1024
SWIGLU_LIMIT = 7.0
SWIGLU_ALPHA = 1.702
EPS = 1e-5

LANES = 128
SUBLANES = 8
ROW_TILES = D_MODEL // LANES
VMEM_LIMIT = 56 * 1024 * 1024

TM_PROJ = 256
TQ = 256
FOX_HEADS_PER_STEP = 4
GLA_ROWS = 512
TM_EXP = 256
SLOT_BLOCK = 8192
SCALAR_UNROLL = 8
TC_COMBINE = 256
QK_WIDTH = FOX_HEADS * LANES
F_COPIES = (0, FOX_HEADS + GLA_GATE_RANK, 2 * FOX_HEADS + GLA_GATE_RANK)
F_COPY2, F_COPY3 = F_COPIES[1:]
VT_ROWS = FOX_HEAD_DIM + 16
NEG_BIG = -1e30
SKIP_MARGIN = 160.0
NORM_SLACK = 1.02
LOG2E = 1.4426950408889634


def _log_sigmoid(z):
    return jnp.minimum(z, 0.0) - jnp.log1p(jnp.exp(-jnp.abs(z)))


def _split3(a):
    p1 = a.astype(BF16)
    r1 = a - p1.astype(F32)
    p2 = r1.astype(BF16)
    r2 = r1 - p2.astype(F32)
    return p1, p2, r2.astype(BF16)


def _dot(a, b):
    return jnp.dot(a, b, preferred_element_type=F32)


def _dot_nt(a, b):
    return lax.dot_general(a, b, (((1,), (1,)), ((), ())), preferred_element_type=F32)


def _dot_tn(a, b):
    return lax.dot_general(a, b, (((0,), (0,)), ((), ())), preferred_element_type=F32)


def _dot3_exact_lhs(tri, a):
    p1, p2, p3 = _split3(a)
    return _dot(tri, p1) + _dot(tri, p2) + _dot(tri, p3)


def _dot_hi(a, b_hi, b_lo):
    a_hi = a.astype(BF16)
    a_lo = (a - a_hi.astype(F32)).astype(BF16)
    return _dot(a_hi, b_hi) + _dot(a_lo, b_hi) + _dot(a_hi, b_lo)


def _rms(x, g):
    return x * lax.rsqrt(jnp.mean(x * x, axis=-1, keepdims=True) + EPS) * g


def _in_proj_kernel(x_ref, g_ref, wm_ref, wsh_ref, wsl_ref, fb_ref, guh_ref, gul_ref, gb_ref,
                    selq_ref, selk_ref, oneq_ref, onek_ref, seln_ref,
                    qa_ref, ka_ref, vt_ref, gq_ref, gk_ref, gv_ref, go_ref, la_ref, stat_ref, carry_ref):
    tm = x_ref.shape[0]

    @pl.when(pl.program_id(1) == 0)
    def _():
        carry_ref[...] = jnp.zeros_like(carry_ref)

    n = _rms(x_ref[...], g_ref[...])
    main = _dot(n.astype(BF16), wm_ref[...])

    small = _dot_hi(n, wsh_ref[...], wsl_ref[...])

    ls = _log_sigmoid(small + fb_ref[...])
    row = lax.broadcasted_iota(jnp.int32, (tm, tm), 0)
    col = lax.broadcasted_iota(jnp.int32, (tm, tm), 1)
    tri = jnp.where(row >= col, 1.0, 0.0).astype(BF16)
    cum = _dot3_exact_lhs(tri, ls) + carry_ref[...]
    carry_ref[...] = cum[tm - 1:tm, :]

    lane = lax.broadcasted_iota(jnp.int32, (tm, LANES), 1)
    p1, p2, p3 = _split3(cum * LOG2E)
    zero = jnp.zeros_like(p1)
    comb = jnp.where(lane < FOX_HEADS, p1,
                     jnp.where((lane >= F_COPY2) & (lane < F_COPY2 + FOX_HEADS), p2,
                               jnp.where((lane >= F_COPY3) & (lane < F_COPY3 + FOX_HEADS), p3, zero)))
    aug_q = _dot(comb, selq_ref[...]) + oneq_ref[...]
    aug_k = _dot(comb, selk_ref[...]) + onek_ref[...]
    lane1 = lax.broadcasted_iota(jnp.int32, (1, LANES), 1)
    q_scale = jnp.where(lane1 < FOX_HEAD_DIM, LOG2E * FOX_HEAD_DIM ** -0.5, 0.0)
    k_scale = jnp.where(lane1 < FOX_HEAD_DIM, 1.0, 0.0)
    ones_row = jnp.where(lax.broadcasted_iota(jnp.int32, (VT_ROWS - FOX_HEAD_DIM, tm), 0) == 0, 1.0, 0.0)
    for p in range(FOX_HEADS // 2):
        qc = main[:, p * LANES:(p + 1) * LANES]
        kc = main[:, FOX_WIDTH + p * LANES:FOX_WIDTH + (p + 1) * LANES]
        vc_t = main[:, 2 * FOX_WIDTH + p * LANES:2 * FOX_WIDTH + (p + 1) * LANES].T
        for hh in range(2):
            h = 2 * p + hh
            hs = slice(h * LANES, (h + 1) * LANES)
            if hh:
                qc = pltpu.roll(qc, FOX_HEAD_DIM, 1)
                kc = pltpu.roll(kc, FOX_HEAD_DIM, 1)
            qa_ref[:, hs] = (qc * q_scale + aug_q[:, hs]).astype(BF16)
            ka_ref[:, hs] = (kc * k_scale + aug_k[:, hs]).astype(BF16)
            vt_ref[h] = jnp.concatenate(
                [vc_t[hh * FOX_HEAD_DIM:(hh + 1) * FOX_HEAD_DIM], ones_row], axis=0).astype(BF16)
    col1 = lax.broadcasted_iota(jnp.int32, (1, 2 * FOX_WIDTH), 1)
    qk = main[:, 0:2 * FOX_WIDTH] * jnp.where(col1 < FOX_WIDTH, LOG2E * FOX_HEAD_DIM ** -0.5, 1.0)
    norms2 = jnp.max(_dot((qk * qk).astype(BF16), seln_ref[...]), axis=0, keepdims=True)
    f2 = cum * LOG2E
    sub = lax.broadcasted_iota(jnp.int32, (SUBLANES, LANES), 0)
    stat_ref[...] = jnp.where(sub == 0, norms2,
                              jnp.where(sub == 1, f2[0:1, :], jnp.where(sub == 2, f2[tm - 1:tm, :], 0.0)))

    o = 3 * FOX_WIDTH
    gq_ref[...] = (main[:, o:o + GLA_KEY_WIDTH] * (GLA_KEY_DIM ** -0.5)).astype(BF16); o += GLA_KEY_WIDTH
    gk_ref[...] = main[:, o:o + GLA_KEY_WIDTH]; o += GLA_KEY_WIDTH
    gv_ref[...] = main[:, o:o + GLA_VAL_WIDTH].astype(BF16); o += GLA_VAL_WIDTH
    go_ref[...] = main[:, o:o + GLA_VAL_WIDTH]

    z = _dot_hi(small, guh_ref[...], gul_ref[...]) + gb_ref[...]
    la_ref[...] = _log_sigmoid(z) * (1.0 / GLA_GATE_TAU)


def _aug_constants():
    selq = np.zeros((LANES, QK_WIDTH), np.float32)
    selk = np.zeros((LANES, QK_WIDTH), np.float32)
    oneq = np.zeros((1, QK_WIDTH), np.float32)
    onek = np.zeros((1, QK_WIDTH), np.float32)
    for h in range(FOX_HEADS):
        base = h * LANES + FOX_HEAD_DIM
        for i, copy in enumerate(F_COPIES):
            selq[copy + h, base + i] = 1.0
            onek[0, base + i] = 1.0
            selk[copy + h, base + 3 + i] = -1.0
            oneq[0, base + 3 + i] = 1.0
    seln = np.zeros((2 * FOX_WIDTH, LANES), np.float32)
    for j in range(2 * FOX_WIDTH):
        seln[j, j // FOX_HEAD_DIM] = 1.0
    return (jnp.asarray(selq, BF16), jnp.asarray(selk, BF16), jnp.asarray(oneq), jnp.asarray(onek),
            jnp.asarray(seln, BF16))


def _in_proj(x2, norm_g, w_main, ws_hi, ws_lo, fb_pad, gu_hi, gu_lo, gb, batch, seq):
    t = batch * seq
    tm = TM_PROJ
    nj = seq // tm
    selq, selk, oneq, onek, seln = _aug_constants()
    rows = lambda w: pl.BlockSpec((tm, w), lambda b, j: (b * nj + j, 0))
    full = lambda a: pl.BlockSpec(a.shape, lambda b, j: (0,) * a.ndim)
    out_shape = [
        jax.ShapeDtypeStruct((t, QK_WIDTH), BF16),
        jax.ShapeDtypeStruct((t, QK_WIDTH), BF16),
        jax.ShapeDtypeStruct((FOX_HEADS, VT_ROWS, t), BF16),
        jax.ShapeDtypeStruct((t, GLA_KEY_WIDTH), BF16),
        jax.ShapeDtypeStruct((t, GLA_KEY_WIDTH), F32),
        jax.ShapeDtypeStruct((t, GLA_VAL_WIDTH), BF16),
        jax.ShapeDtypeStruct((t, GLA_VAL_WIDTH), F32),
        jax.ShapeDtypeStruct((t, GLA_KEY_WIDTH), F32),
        jax.ShapeDtypeStruct((batch * nj * SUBLANES, LANES), F32),
    ]
    out_specs = [rows(QK_WIDTH), rows(QK_WIDTH),
                 pl.BlockSpec((FOX_HEADS, VT_ROWS, tm), lambda b, j: (0, 0, b * nj + j)),
                 rows(GLA_KEY_WIDTH), rows(GLA_KEY_WIDTH), rows(GLA_VAL_WIDTH), rows(GLA_VAL_WIDTH),
                 rows(GLA_KEY_WIDTH),
                 pl.BlockSpec((SUBLANES, LANES), lambda b, j: (b * nj + j, 0))]
    args = (x2, norm_g, w_main, ws_hi, ws_lo, fb_pad, gu_hi, gu_lo, gb, selq, selk, oneq, onek, seln)
    return pl.pallas_call(
        _in_proj_kernel,
        grid=(batch, nj),
        in_specs=[rows(D_MODEL)] + [full(a) for a in args[1:]],
        out_specs=out_specs,
        out_shape=out_shape,
        scratch_shapes=[pltpu.VMEM((1, LANES), F32)],
        compiler_params=pltpu.CompilerParams(
            dimension_semantics=("arbitrary", "arbitrary"), vmem_limit_bytes=VMEM_LIMIT),
        name="in_proj",
    )(*args)


def _fox_kernel(lo_ref, q_ref, k_ref, vt_ref, o_ref, acc_ref, m_ref, sa_ref, ma_ref, sb_ref, mb_ref):
    tq = q_ref.shape[0]
    tk = tq
    nh = q_ref.shape[1] // LANES
    i = pl.program_id(2)

    m_ref[...] = jnp.full_like(m_ref, -jnp.inf)
    acc_ref[...] = jnp.zeros_like(acc_ref)
    slots = ((sa_ref, ma_ref), (sb_ref, mb_ref))

    def stage_scores(block, slot, masked=False):
        start = pl.multiple_of(block * tk, tk)
        for h in range(nh):
            s = _dot_nt(k_ref[pl.ds(start, tk), h * LANES:(h + 1) * LANES],
                        q_ref[:, h * LANES:(h + 1) * LANES])
            if masked:
                key = lax.broadcasted_iota(jnp.int32, (tk, tq), 0)
                qry = lax.broadcasted_iota(jnp.int32, (tk, tq), 1)
                s = jnp.where(key <= qry, s, -jnp.inf)
            slots[slot][0][h] = s
            slots[slot][1][h] = jnp.max(s.reshape(tk // SUBLANES, SUBLANES, tq), axis=0)

    def stage_accumulate(block, slot):
        start = pl.multiple_of(block * tk, tk)
        probs, alphas = [], []
        for h in range(nh):
            m_prev = m_ref[h]
            m_new = jnp.maximum(m_prev, jnp.max(slots[slot][1][h], axis=0, keepdims=True))
            probs.append(jnp.exp2(slots[slot][0][h] - m_new[0:1]).astype(BF16))
            alphas.append(jnp.exp2(m_prev - m_new)[0:1])
            m_ref[h] = m_new
        for h in range(nh):
            acc_ref[h] = alphas[h] * acc_ref[h] + _dot(vt_ref[h, :, pl.ds(start, tk)], probs[h])

    lo = lo_ref[(pl.program_id(0) * pl.num_programs(1) + pl.program_id(1)) * pl.num_programs(2) + i]
    n = i - lo

    def processed(m):
        return jnp.where(m == 0, i, lo + m - 1)

    stage_scores(i, 0, masked=True)
    pairs = n // 2

    def body(mm, c):
        m = 2 * mm
        stage_scores(lo + m, 1)
        stage_accumulate(processed(m), 0)
        stage_scores(lo + m + 1, 0)
        stage_accumulate(lo + m, 1)
        return c

    lax.fori_loop(0, pairs, body, 0)
    last = 2 * pairs

    @pl.when(n > last)
    def _():
        stage_scores(lo + last, 1)
        stage_accumulate(processed(last), 0)
        stage_accumulate(lo + last, 1)

    @pl.when(n == last)
    def _():
        stage_accumulate(processed(last), 0)

    o_t = jnp.concatenate(
        [acc_ref[h, 0:FOX_HEAD_DIM, :] / acc_ref[h, FOX_HEAD_DIM:FOX_HEAD_DIM + 1, :]
         for h in range(nh)], axis=0)
    o_ref[...] = o_t.T.astype(o_ref.dtype)


def _fox(first_block, qa, ka, vt, batch, seq):
    t = batch * seq
    nq = seq // TQ
    nh = FOX_HEADS_PER_STEP
    groups = FOX_HEADS // nh
    grid_spec = pltpu.PrefetchScalarGridSpec(
        num_scalar_prefetch=1,
        grid=(batch, groups, nq),
        in_specs=[
            pl.BlockSpec((TQ, nh * LANES), lambda b, g, i, lo: (b * nq + i, g)),
            pl.BlockSpec((seq, nh * LANES), lambda b, g, i, lo: (b, g)),
            pl.BlockSpec((nh, VT_ROWS, seq), lambda b, g, i, lo: (g, 0, b)),
        ],
        out_specs=pl.BlockSpec((TQ, nh * FOX_HEAD_DIM), lambda b, g, i, lo: (b * nq + i, g)),
        scratch_shapes=[pltpu.VMEM((nh, VT_ROWS, TQ), F32),
                        pltpu.VMEM((nh, SUBLANES, TQ), F32),
                        pltpu.VMEM((nh, TQ, TQ), F32),
                        pltpu.VMEM((nh, SUBLANES, TQ), F32),
                        pltpu.VMEM((nh, TQ, TQ), F32),
                        pltpu.VMEM((nh, SUBLANES, TQ), F32)],
    )
    return pl.pallas_call(
        _fox_kernel,
        grid_spec=grid_spec,
        out_shape=jax.ShapeDtypeStruct((t, FOX_WIDTH), BF16),
        compiler_params=pltpu.CompilerParams(
            dimension_semantics=("parallel", "parallel", "arbitrary"), vmem_limit_bytes=VMEM_LIMIT),
        name="fox_attention",
    )(first_block, qa, ka, vt)


def _gla_kernel(q_ref, k_ref, la_ref, v_ref, go_ref, ng_ref, o_ref, st_ref, obuf_ref):
    rows = q_ref.shape[0]
    pairs = GLA_HEADS // 2

    @pl.when(pl.program_id(1) == 0)
    def _():
        st_ref[...] = jnp.zeros_like(st_ref)

    r = lax.broadcasted_iota(jnp.int32, (CHUNK, CHUNK), 0)
    c = lax.broadcasted_iota(jnp.int32, (CHUNK, CHUNK), 1)
    tri = jnp.where(r >= c, 1.0, 0.0).astype(BF16)
    vrow = lax.broadcasted_iota(jnp.int32, (2 * GLA_VAL_DIM, 2 * GLA_KEY_DIM), 0)
    kcol = lax.broadcasted_iota(jnp.int32, (2 * GLA_VAL_DIM, 2 * GLA_KEY_DIM), 1)
    same_head = (vrow >= GLA_VAL_DIM) == (kcol >= GLA_KEY_DIM)

    for ch in range(rows // CHUNK):
        sl = slice(ch * CHUNK, (ch + 1) * CHUNK)
        for p in range(pairs):
            ks = slice(p * 2 * GLA_KEY_DIM, (p + 1) * 2 * GLA_KEY_DIM)
            vs = slice(p * 2 * GLA_VAL_DIM, (p + 1) * 2 * GLA_VAL_DIM)
            cum = _dot3_exact_lhs(tri, la_ref[sl, ks])
            tot = cum[CHUNK - 1:CHUNK, :]
            kd = (k_ref[sl, ks] * jnp.exp(tot - cum)).astype(BF16)
            upd_t = _dot_tn(v_ref[sl, vs], kd)
            st = st_ref[p] * jnp.exp(tot) + jnp.where(same_head, upd_t, 0.0)
            st_ref[p] = st
            obuf_ref[sl, vs] = _dot_nt(q_ref[sl, ks], st.astype(BF16))

    o = obuf_ref[...]
    go = go_ref[...]
    ng = ng_ref[...]
    halves = []
    for h in range(GLA_HEADS):
        hs = slice(h * GLA_VAL_DIM, (h + 1) * GLA_VAL_DIM)
        oh = o[:, hs]
        oh = oh * lax.rsqrt(jnp.mean(oh * oh, axis=-1, keepdims=True) + EPS) * ng
        gh = go[:, hs]
        halves.append(oh * (gh * jax.nn.sigmoid(gh)))
    o_ref[...] = jnp.concatenate(halves, axis=1).astype(o_ref.dtype)


def _gla(gq, gk, la, gv, go, ng, batch, seq):
    t = batch * seq
    nr = seq // GLA_ROWS
    kspec = pl.BlockSpec((GLA_ROWS, GLA_KEY_WIDTH), lambda b, i: (b * nr + i, 0))
    vspec = pl.BlockSpec((GLA_ROWS, GLA_VAL_WIDTH), lambda b, i: (b * nr + i, 0))
    return pl.pallas_call(
        _gla_kernel,
        grid=(batch, nr),
        in_specs=[kspec, kspec, kspec, vspec, vspec,
                  pl.BlockSpec((1, GLA_VAL_DIM), lambda b, i: (0, 0))],
        out_specs=vspec,
        out_shape=jax.ShapeDtypeStruct((t, GLA_VAL_WIDTH), BF16),
        scratch_shapes=[pltpu.VMEM((GLA_HEADS // 2, 2 * GLA_VAL_DIM, 2 * GLA_KEY_DIM), F32),
                        pltpu.VMEM((GLA_ROWS, GLA_VAL_WIDTH), F32)],
        compiler_params=pltpu.CompilerParams(
            dimension_semantics=("parallel", "arbitrary"), vmem_limit_bytes=VMEM_LIMIT),
        name="gla",
    )(gq, gk, la, gv, go, ng)


def _out_proj_kernel(x_ref, fox_ref, gla_ref, wf_ref, wg_ref, g2_ref, rwh_ref, rwl_ref, rb_ref,
                     h_ref, n2_ref, ti_ref, gate_ref, rank_ref, cnt_ref, carry_ref):
    tm = x_ref.shape[0]

    @pl.when(pl.program_id(0) == 0)
    def _():
        carry_ref[...] = jnp.zeros_like(carry_ref)

    h = x_ref[...] + _dot(fox_ref[...], wf_ref[...]) + _dot(gla_ref[...], wg_ref[...])
    h_ref[...] = h
    n2 = _rms(h, g2_ref[...])
    for c in range(ROW_TILES):
        n2_ref[pl.ds(c, tm, stride=ROW_TILES), :] = n2[:, c * LANES:(c + 1) * LANES]

    logits = _dot_hi(n2, rwh_ref[...], rwl_ref[...]) + rb_ref[...]
    work = logits.T[0:N_EXPERTS, :]
    eidx = lax.broadcasted_iota(jnp.int32, (N_EXPERTS, tm), 0)
    vals, idxs, onehots = [], [], []
    for _ in range(TOP_K):
        m = jnp.max(work, axis=0, keepdims=True)
        idx = jnp.min(jnp.where(work == m, eidx, N_EXPERTS), axis=0, keepdims=True)
        oh = eidx == idx
        vals.append(m)
        idxs.append(idx)
        onehots.append(oh)
        work = jnp.where(oh, -jnp.inf, work)
    exps = [jnp.exp(v - vals[0]) for v in vals]
    denom = exps[0] + exps[1] + exps[2] + exps[3]

    chosen = jnp.zeros((N_EXPERTS, tm), F32)
    for oh in onehots:
        chosen = chosen + jnp.where(oh, 1.0, 0.0)
    chosen = chosen.astype(BF16)
    row = lax.broadcasted_iota(jnp.int32, (tm, tm), 0)
    col = lax.broadcasted_iota(jnp.int32, (tm, tm), 1)
    earlier = jnp.where(row < col, 1.0, 0.0).astype(BF16)
    carry = carry_ref[...]
    before = _dot(chosen, earlier) + carry
    carry = carry + _dot(chosen, jnp.ones((tm, tm), BF16))
    carry_ref[...] = carry
    cnt_ref[...] = carry[:, 0:LANES].astype(jnp.int32)

    sub = lax.broadcasted_iota(jnp.int32, (SUBLANES, tm), 0)
    ti = jnp.zeros((SUBLANES, tm), jnp.int32)
    gates = jnp.zeros((SUBLANES, tm), F32)
    ranks = jnp.zeros((SUBLANES, tm), F32)
    for kk in range(TOP_K):
        sel = sub == kk
        ti = jnp.where(sel, idxs[kk], ti)
        gates = jnp.where(sel, exps[kk] / denom, gates)
        rk = jnp.sum(jnp.where(onehots[kk], before, 0.0), axis=0, keepdims=True)
        ranks = jnp.where(sel, rk, ranks)
    ti_ref[...] = ti
    gate_ref[...] = gates
    rank_ref[...] = ranks.astype(jnp.int32)


def _out_proj(x2, fox, gla, wf, wg, g2, rw_hi, rw_lo, rb_pad):
    t = x2.shape[0]
    tm = TM_PROJ
    rows = lambda w: pl.BlockSpec((tm, w), lambda i: (i, 0))
    full = lambda a: pl.BlockSpec(a.shape, lambda i: (0,) * a.ndim)
    out_shape = [
        jax.ShapeDtypeStruct((t, D_MODEL), F32),
        jax.ShapeDtypeStruct((t * ROW_TILES, LANES), F32),
        jax.ShapeDtypeStruct((SUBLANES, t), jnp.int32),
        jax.ShapeDtypeStruct((SUBLANES, t), F32),
        jax.ShapeDtypeStruct((SUBLANES, t), jnp.int32),
        jax.ShapeDtypeStruct((N_EXPERTS, LANES), jnp.int32),
    ]
    per_token = pl.BlockSpec((SUBLANES, tm), lambda i: (0, i))
    out_specs = [rows(D_MODEL), pl.BlockSpec((tm * ROW_TILES, LANES), lambda i: (i, 0)),
                 per_token, per_token, per_token,
                 pl.BlockSpec((N_EXPERTS, LANES), lambda i: (0, 0))]
    return pl.pallas_call(
        _out_proj_kernel,
        grid=(t // tm,),
        in_specs=[rows(D_MODEL), rows(FOX_WIDTH), rows(GLA_VAL_WIDTH), full(wf), full(wg), full(g2),
                  full(rw_hi), full(rw_lo), full(rb_pad)],
        out_specs=out_specs,
        out_shape=out_shape,
        scratch_shapes=[pltpu.VMEM((N_EXPERTS, tm), F32)],
        compiler_params=pltpu.CompilerParams(
            dimension_semantics=("arbitrary",), vmem_limit_bytes=VMEM_LIMIT),
        name="out_proj_router",
    )(x2, fox, gla, wf, wg, g2, rw_hi, rw_lo, rb_pad)


def _slot_sources_kernel(starts_ref, cnt_ref, ends_ref, pos_ref, src_ref):
    g = pl.program_id(0)
    nblk = pos_ref.shape[0]
    n_slots = src_ref.shape[0]
    n_pairs = nblk * pl.num_programs(0)

    @pl.when(g == 0)
    def _():
        def fill(lo, hi, dump):
            def body(r, d):
                src_ref[r] = d
                return d + 1
            return lax.fori_loop(lo, hi, body, dump)

        def per_expert(e, dump):
            return fill(starts_ref[e] + cnt_ref[e], ends_ref[e], dump)

        dump = lax.fori_loop(0, N_EXPERTS, per_expert, jnp.int32(n_pairs))
        fill(ends_ref[N_EXPERTS - 1], n_slots, dump)

    base = g * nblk

    def body(j, c):
        for u in range(SCALAR_UNROLL):
            p = j * SCALAR_UNROLL + u
            src_ref[pos_ref[p]] = base + p
        return c

    lax.fori_loop(0, nblk // SCALAR_UNROLL, body, 0)


def _slot_sources(starts, cnt, ends, pos_km, n_slots):
    n_pairs = pos_km.shape[0]
    blk = SLOT_BLOCK
    grid_spec = pltpu.PrefetchScalarGridSpec(
        num_scalar_prefetch=3,
        grid=(n_pairs // blk,),
        in_specs=[pl.BlockSpec((blk,), lambda g, s, c, e: (g,), memory_space=pltpu.SMEM)],
        out_specs=pl.BlockSpec((n_slots,), lambda g, s, c, e: (0,), memory_space=pltpu.SMEM),
    )
    return pl.pallas_call(
        _slot_sources_kernel,
        grid_spec=grid_spec,
        out_shape=jax.ShapeDtypeStruct((n_slots,), jnp.int32),
        compiler_params=pltpu.CompilerParams(dimension_semantics=("arbitrary",)),
        name="slot_sources",
    )(starts, cnt, ends, pos_km)


def _tile_rows(ref, row):
    return ref.at[pl.ds(pl.multiple_of(row * ROW_TILES, ROW_TILES), ROW_TILES)]


def _experts_kernel(te_ref, nu_ref, src_ref, first_ref, par_ref, nxt_ref,
                    n2_ref, wi_ref, bi0_ref, bi1_ref, wo_ref, bo0_ref, bo1_ref, yg_ref,
                    wib_ref, wob_ref, wif_ref, wof_ref, xbuf_ref, ybuf_ref, gsem, ssem, wsem):
    step = pl.program_id(0)
    n_tiles = 2 * pl.num_programs(0)
    tm = xbuf_ref.shape[1] // ROW_TILES
    n_tokens = n2_ref.shape[0] // ROW_TILES
    nu = nu_ref[0]

    def issue_gather(tile, slot):
        base = jnp.minimum(tile, n_tiles - 1) * tm
        for r in range(tm):
            token = src_ref[base + r] & (n_tokens - 1)
            pltpu.make_async_copy(
                _tile_rows(n2_ref, token), xbuf_ref.at[slot, pl.ds(r * ROW_TILES, ROW_TILES)],
                gsem.at[slot]).start(priority=r % 2)

    def wait_gather(slot):
        pltpu.make_async_copy(
            n2_ref.at[pl.ds(0, tm * ROW_TILES)], xbuf_ref.at[slot], gsem.at[slot]).wait()

    def issue_scatter(tile, slot):
        base = tile * tm
        for r in range(tm):
            pltpu.make_async_copy(
                ybuf_ref.at[slot, pl.ds(r * ROW_TILES, ROW_TILES)],
                _tile_rows(yg_ref, src_ref[base + r]), ssem.at[slot]).start(priority=r % 2)

    def wait_scatter(slot):
        pltpu.make_async_copy(
            ybuf_ref.at[slot], yg_ref.at[pl.ds(0, tm * ROW_TILES)], ssem.at[slot]).wait()

    def weight_copies(expert, slot):
        return (pltpu.make_async_copy(wi_ref.at[expert], wif_ref.at[slot], wsem.at[slot, 0]),
                pltpu.make_async_copy(wo_ref.at[expert], wof_ref.at[slot], wsem.at[slot, 1]))

    @pl.when(step == 0)
    def _():
        issue_gather(0, 0)
        issue_gather(1, 1)
        for c in weight_copies(te_ref[0], 0):
            c.start()

    def run_tile(i, slot, bi_ref, bo_ref):
        @pl.when(first_ref[i] == 1)
        def _():
            wslot = par_ref[i]

            @pl.when(nxt_ref[i] >= 0)
            def _():
                for c in weight_copies(nxt_ref[i], 1 - wslot):
                    c.start()

            for c in weight_copies(te_ref[i], wslot):
                c.wait()
            wib_ref[...] = wif_ref[wslot].astype(BF16)
            wob_ref[...] = wof_ref[wslot].astype(BF16)

        @pl.when((i >= 2) & (i < nu))
        def _():
            wait_scatter(slot)

        @pl.when(i < nu)
        def _():
            wait_gather(slot)
            x = jnp.concatenate(
                [xbuf_ref[slot, pl.ds(c, tm, stride=ROW_TILES), :] for c in range(ROW_TILES)],
                axis=1).astype(BF16)
            issue_gather(i + 2, slot)
            h = _dot(x, wib_ref[...]) + bi_ref[0]
            gate = jnp.minimum(h[:, :D_FF], SWIGLU_LIMIT)
            lin = jnp.clip(h[:, D_FF:], -SWIGLU_LIMIT, SWIGLU_LIMIT)
            a = (lin + 1.0) * (gate * jax.nn.sigmoid(SWIGLU_ALPHA * gate))
            y = _dot(a.astype(BF16), wob_ref[...]) + bo_ref[0]
            for c in range(ROW_TILES):
                ybuf_ref[slot, pl.ds(c, tm, stride=ROW_TILES), :] = y[:, c * LANES:(c + 1) * LANES]
            issue_scatter(i, slot)

        @pl.when(i == nu - 1)
        def _():
            wait_gather(1 - slot)
            wait_gather(slot)
            wait_scatter(slot)

            @pl.when(i >= 1)
            def _():
                wait_scatter(1 - slot)

        @pl.when(i >= nu)
        def _():
            ybuf_ref[slot] = jnp.zeros_like(ybuf_ref[slot])
            tail = pltpu.make_async_copy(
                ybuf_ref.at[slot],
                yg_ref.at[pl.ds(pl.multiple_of(src_ref[i * tm] * ROW_TILES, ROW_TILES), tm * ROW_TILES)],
                ssem.at[slot])
            tail.start()
            tail.wait()

    run_tile(2 * step, 0, bi0_ref, bo0_ref)
    run_tile(2 * step + 1, 1, bi1_ref, bo1_ref)


def _experts(tile_expert, num_used, src, group_first, group_parity, next_expert,
             n2_tiles, w_in, b_in, w_out, b_out):
    n_tiles = tile_expert.shape[0]
    assert n_tiles % 2 == 0
    tm = TM_EXP
    bias = lambda w, k: pl.BlockSpec((1, 1, w), lambda s, te, *_: (te[2 * s + k], 0, 0))
    grid_spec = pltpu.PrefetchScalarGridSpec(
        num_scalar_prefetch=6,
        grid=(n_tiles // 2,),
        in_specs=[pl.BlockSpec(memory_space=pl.ANY), pl.BlockSpec(memory_space=pl.ANY),
                  bias(2 * D_FF, 0), bias(2 * D_FF, 1),
                  pl.BlockSpec(memory_space=pl.ANY), bias(D_MODEL, 0), bias(D_MODEL, 1)],
        out_specs=pl.BlockSpec(memory_space=pl.ANY),
        scratch_shapes=[pltpu.VMEM((D_MODEL, 2 * D_FF), BF16), pltpu.VMEM((D_FF, D_MODEL), BF16),
                        pltpu.VMEM((2, D_MODEL, 2 * D_FF), F32),
                        pltpu.VMEM((2, D_FF, D_MODEL), F32),
                        pltpu.VMEM((2, tm * ROW_TILES, LANES), F32),
                        pltpu.VMEM((2, tm * ROW_TILES, LANES), F32),
                        pltpu.SemaphoreType.DMA((2,)), pltpu.SemaphoreType.DMA((2,)),
                        pltpu.SemaphoreType.DMA((2, 2))],
    )
    return pl.pallas_call(
        _experts_kernel,
        grid_spec=grid_spec,
        out_shape=jax.ShapeDtypeStruct((n_tiles * tm * ROW_TILES, LANES), F32),
        compiler_params=pltpu.CompilerParams(
            dimension_semantics=("arbitrary",), vmem_limit_bytes=VMEM_LIMIT),
        name="experts",
    )(tile_expert, num_used, src, group_first, group_parity, next_expert,
      n2_tiles, w_in, b_in, b_in, w_out, b_out, b_out)


def _combine_kernel(gate_ref, h_ref, fg_ref, y0_ref, y1_ref, y2_ref, y3_ref, o_ref):
    tc = h_ref.shape[0]
    gates = gate_ref[...].T
    h = h_ref[...]
    for kk, y_ref in enumerate((y0_ref, y1_ref, y2_ref, y3_ref)):
        yk = jnp.concatenate(
            [y_ref[pl.ds(c, tc, stride=ROW_TILES), :] for c in range(ROW_TILES)], axis=1)
        h = h + gates[:, kk:kk + 1] * yk
    o_ref[...] = _rms(h, fg_ref[...])


def _combine(gates, h1, final_g, yg):
    t = h1.shape[0]
    tc = TC_COMBINE
    nblk = t // tc

    def pair_rows(kk):
        return pl.BlockSpec((tc * ROW_TILES, LANES), lambda i: (kk * nblk + i, 0))

    return pl.pallas_call(
        _combine_kernel,
        grid=(nblk,),
        in_specs=[pl.BlockSpec((SUBLANES, tc), lambda i: (0, i)),
                  pl.BlockSpec((tc, D_MODEL), lambda i: (i, 0)),
                  pl.BlockSpec((1, D_MODEL), lambda i: (0, 0))] + [pair_rows(kk) for kk in range(TOP_K)],
        out_specs=pl.BlockSpec((tc, D_MODEL), lambda i: (i, 0)),
        out_shape=jax.ShapeDtypeStruct((t, D_MODEL), F32),
        compiler_params=pltpu.CompilerParams(
            dimension_semantics=("parallel",), vmem_limit_bytes=VMEM_LIMIT),
        name="combine",
    )(gates, h1, final_g, yg, yg, yg, yg)


def _hi_lo(w):
    hi = w.astype(BF16)
    return hi, (w - hi.astype(F32)).astype(BF16)


def _first_needed_block(stats, batch, seq):
    assert TM_PROJ == TQ
    nq = seq // TQ
    st = stats.reshape(batch, nq, SUBLANES, LANES)
    q_norm = jnp.sqrt(st[:, :, 0, 0:FOX_HEADS])
    k_norm = jnp.sqrt(jnp.max(st[:, :, 0, FOX_HEADS:2 * FOX_HEADS], axis=1))
    f_first = st[:, :, 1, 0:FOX_HEADS]
    f_last = st[:, :, 2, 0:FOX_HEADS]
    bound = SKIP_MARGIN + 2.0 * NORM_SLACK * q_norm * k_norm[:, None, :]
    decay = f_first[:, :, None, :] - f_last[:, None, :, :]
    tile = jnp.arange(nq, dtype=jnp.int32)
    needed = (tile[None, :] < tile[:, None])[None, :, :, None] & (decay >= -bound[:, :, None, :])
    lo = jnp.min(jnp.where(needed, tile[None, None, :, None], tile[None, :, None, None]), axis=2)
    lo = jnp.min(lo.reshape(batch, nq, FOX_HEADS // FOX_HEADS_PER_STEP, FOX_HEADS_PER_STEP), axis=-1)
    return lo.transpose(0, 2, 1).reshape(-1).astype(jnp.int32)


def _layer(x2, batch, seq, norm1_g, w_in, fox_f_bias, gla_gate_up, gla_gate_bias, gla_norm_g, w_out,
           norm2_g, router_w, router_b, exp_w_in, exp_b_in, exp_w_out, exp_b_out, final_g):
    t = batch * seq
    o = 0
    segs = {}
    for name, width in (("fq", FOX_WIDTH), ("fk", FOX_WIDTH), ("fv", FOX_WIDTH), ("ff", FOX_HEADS),
                        ("gq", GLA_KEY_WIDTH), ("gk", GLA_KEY_WIDTH), ("gv", GLA_VAL_WIDTH),
                        ("gl", GLA_GATE_RANK), ("go", GLA_VAL_WIDTH)):
        segs[name] = w_in[:, o:o + width]
        o += width
    w_main = jnp.concatenate(
        [segs[n] for n in ("fq", "fk", "fv", "gq", "gk", "gv", "go")], axis=1).astype(BF16)
    n_small = FOX_HEADS + GLA_GATE_RANK
    w_small = jnp.concatenate(
        [segs["ff"], segs["gl"], segs["ff"], segs["ff"],
         jnp.zeros((D_MODEL, LANES - n_small - 2 * FOX_HEADS), F32)], axis=1)
    ws_hi, ws_lo = _hi_lo(w_small)
    fb_pad = jnp.concatenate(
        [fox_f_bias, jnp.zeros((GLA_GATE_RANK,), F32), fox_f_bias, fox_f_bias,
         jnp.zeros((LANES - n_small - 2 * FOX_HEADS,), F32)]).reshape(1, LANES)
    gu_pad = jnp.pad(gla_gate_up, ((FOX_HEADS, LANES - n_small), (0, 0)))
    gu_hi, gu_lo = _hi_lo(gu_pad)

    qa, ka, vt, gq, gk, gv, go, la, stats = _in_proj(
        x2, norm1_g.reshape(1, D_MODEL), w_main, ws_hi, ws_lo, fb_pad, gu_hi, gu_lo,
        gla_gate_bias.reshape(1, GLA_KEY_WIDTH), batch, seq)

    fox = _fox(_first_needed_block(stats, batch, seq), qa, ka, vt, batch, seq)
    gla = _gla(gq, gk, la, gv, go, gla_norm_g.reshape(1, GLA_VAL_DIM), batch, seq)

    w_out_b = w_out.astype(BF16)
    rw_hi, rw_lo = _hi_lo(jnp.pad(router_w, ((0, 0), (0, LANES - N_EXPERTS))))
    rb_pad = jnp.pad(router_b, (0, LANES - N_EXPERTS), constant_values=NEG_BIG).reshape(1, LANES)
    h1, n2_tiles, ti, gates, rank, counts = _out_proj(
        x2, fox, gla, w_out_b[:FOX_WIDTH], w_out_b[FOX_WIDTH:], norm2_g.reshape(1, D_MODEL),
        rw_hi, rw_lo, rb_pad)

    cnt = counts[:, 0]
    padded = ((cnt + TM_EXP - 1) // TM_EXP) * TM_EXP
    ends = jnp.cumsum(padded)
    starts = ends - padded
    expert_ids = jnp.arange(N_EXPERTS, dtype=jnp.int32)
    start_of = jnp.sum(jnp.where(ti[:TOP_K, :, None] == expert_ids, starts, 0), axis=-1)
    pos_km = (start_of + rank[:TOP_K]).reshape(-1).astype(jnp.int32)
    n_tiles = (t * TOP_K) // TM_EXP + N_EXPERTS
    num_used = (ends[-1] // TM_EXP).astype(jnp.int32)
    tile_start = jnp.arange(n_tiles, dtype=jnp.int32) * TM_EXP
    tile_expert = jnp.minimum(
        jnp.sum((tile_start[:, None] >= ends[None, :]).astype(jnp.int32), axis=1), N_EXPERTS - 1)
    last_expert = tile_expert[jnp.maximum(num_used - 1, 0)]
    tile_expert = jnp.where(jnp.arange(n_tiles) < num_used, tile_expert, last_expert).astype(jnp.int32)

    src = _slot_sources(starts.astype(jnp.int32), cnt.astype(jnp.int32), ends.astype(jnp.int32), pos_km,
                        n_tiles * TM_EXP)
    tile_ids = jnp.arange(n_tiles, dtype=jnp.int32)
    group_first = jnp.concatenate(
        [jnp.ones((1,), jnp.int32), (tile_expert[1:] != tile_expert[:-1]).astype(jnp.int32)])
    group_parity = (jnp.cumsum(group_first) - 1) & 1
    first_pos = jnp.where(group_first == 1, tile_ids, n_tiles)
    next_first = jnp.min(
        jnp.where(first_pos[None, :] > tile_ids[:, None], first_pos[None, :], n_tiles), axis=1)
    next_expert = jnp.where(next_first < n_tiles,
                            tile_expert[jnp.minimum(next_first, n_tiles - 1)], -1).astype(jnp.int32)
    yg = _experts(tile_expert, num_used.reshape(1), src, group_first, group_parity.astype(jnp.int32),
                  next_expert, n2_tiles, exp_w_in,
                  exp_b_in.reshape(N_EXPERTS, 1, 2 * D_FF), exp_w_out,
                  exp_b_out.reshape(N_EXPERTS, 1, D_MODEL))
    return _combine(gates, h1, final_g.reshape(1, D_MODEL), yg)


def kernel(x, norm1_g, w_in, fox_f_bias, gla_gate_up, gla_gate_bias, gla_norm_g, w_out, norm2_g,
           router_w, router_b, exp_w_in, exp_b_in, exp_w_out, exp_b_out, final_g):
    batch, seq, d = x.shape
    depth = norm1_g.shape[0]
    assert depth == 1 and d == D_MODEL
    out = _layer(x.reshape(batch * seq, d), batch, seq, norm1_g[0], w_in[0], fox_f_bias[0],
                 gla_gate_up[0], gla_gate_bias[0], gla_norm_g[0], w_out[0], norm2_g[0], router_w[0],
                 router_b[0], exp_w_in[0], exp_b_in[0], exp_w_out[0], exp_b_out[0], final_g)
    return out.reshape(batch, seq, d)
```

```python
import jax
import jax.numpy as jnp
import numpy as np
from jax import lax
from jax.experimental import pallas as pl
from jax.experimental.pallas import tpu as pltpu

F32 = jnp.float32
BF16 = jnp.bfloat16

D_MODEL = 1024
FOX_HEADS = 8
FOX_HEAD_DIM = 64
FOX_WIDTH = 512
GLA_HEADS = 4
GLA_KEY_DIM = 64
GLA_KEY_WIDTH = 256
GLA_VAL_DIM = 128
GLA_VAL_WIDTH = 512
GLA_GATE_RANK = 16
GLA_GATE_TAU = 16.0
CHUNK = 64
N_EXPERTS = 32
TOP_K = 4
D_FF = 1024
SWIGLU_LIMIT = 7.0
SWIGLU_ALPHA = 1.702
EPS = 1e-5

LANES = 128
SUBLANES = 8
ROW_TILES = D_MODEL // LANES
VMEM_LIMIT = 56 * 1024 * 1024

TM_PROJ = 256
TQ = 256
FOX_HEADS_PER_STEP = 4
GLA_ROWS = 512
TM_EXP = 256
SLOT_BLOCK = 8192
SCALAR_UNROLL = 8
TC_COMBINE = 512
QK_WIDTH = FOX_HEADS * LANES
F_COPIES = (0, FOX_HEADS + GLA_GATE_RANK, 2 * FOX_HEADS + GLA_GATE_RANK)
F_COPY2, F_COPY3 = F_COPIES[1:]
VT_ROWS = FOX_HEAD_DIM + 16
NEG_BIG = -1e30
SKIP_MARGIN = 160.0
NORM_SLACK = 1.02
LOG2E = 1.4426950408889634


def _log_sigmoid(z):
    return jnp.minimum(z, 0.0) - jnp.log1p(jnp.exp(-jnp.abs(z)))


def _split3(a):
    p1 = a.astype(BF16)
    r1 = a - p1.astype(F32)
    p2 = r1.astype(BF16)
    r2 = r1 - p2.astype(F32)
    return p1, p2, r2.astype(BF16)


def _dot(a, b):
    return jnp.dot(a, b, preferred_element_type=F32)


def _dot_nt(a, b):
    return lax.dot_general(a, b, (((1,), (1,)), ((), ())), preferred_element_type=F32)


def _dot_tn(a, b):
    return lax.dot_general(a, b, (((0,), (0,)), ((), ())), preferred_element_type=F32)


def _dot3_exact_lhs(tri, a):
    p1, p2, p3 = _split3(a)
    return _dot(tri, p1) + _dot(tri, p2) + _dot(tri, p3)


def _dot_hi(a, b_hi, b_lo):
    a_hi = a.astype(BF16)
    a_lo = (a - a_hi.astype(F32)).astype(BF16)
    return _dot(a_hi, b_hi) + _dot(a_lo, b_hi) + _dot(a_hi, b_lo)


def _rms(x, g):
    return x * lax.rsqrt(jnp.mean(x * x, axis=-1, keepdims=True) + EPS) * g


def _in_proj_kernel(x_ref, g_ref, wm_ref, wsh_ref, wsl_ref, fb_ref, guh_ref, gul_ref, gb_ref,
                    selq_ref, selk_ref, oneq_ref, onek_ref, seln_ref,
                    qa_ref, ka_ref, vt_ref, gq_ref, gk_ref, gv_ref, go_ref, la_ref, stat_ref, carry_ref):
    tm = x_ref.shape[0]

    @pl.when(pl.program_id(1) == 0)
    def _():
        carry_ref[...] = jnp.zeros_like(carry_ref)

    n = _rms(x_ref[...], g_ref[...])
    main = _dot(n.astype(BF16), wm_ref[...])

    small = _dot_hi(n, wsh_ref[...], wsl_ref[...])

    ls = _log_sigmoid(small + fb_ref[...])
    row = lax.broadcasted_iota(jnp.int32, (tm, tm), 0)
    col = lax.broadcasted_iota(jnp.int32, (tm, tm), 1)
    tri = jnp.where(row >= col, 1.0, 0.0).astype(BF16)
    cum = _dot3_exact_lhs(tri, ls) + carry_ref[...]
    carry_ref[...] = cum[tm - 1:tm, :]

    lane = lax.broadcasted_iota(jnp.int32, (tm, LANES), 1)
    p1, p2, p3 = _split3(cum * LOG2E)
    zero = jnp.zeros_like(p1)
    comb = jnp.where(lane < FOX_HEADS, p1,
                     jnp.where((lane >= F_COPY2) & (lane < F_COPY2 + FOX_HEADS), p2,
                               jnp.where((lane >= F_COPY3) & (lane < F_COPY3 + FOX_HEADS), p3, zero)))
    aug_q = _dot(comb, selq_ref[...]) + oneq_ref[...]
    aug_k = _dot(comb, selk_ref[...]) + onek_ref[...]
    lane1 = lax.broadcasted_iota(jnp.int32, (1, LANES), 1)
    q_scale = jnp.where(lane1 < FOX_HEAD_DIM, LOG2E * FOX_HEAD_DIM ** -0.5, 0.0)
    k_scale = jnp.where(lane1 < FOX_HEAD_DIM, 1.0, 0.0)
    ones_row = jnp.where(lax.broadcasted_iota(jnp.int32, (VT_ROWS - FOX_HEAD_DIM, tm), 0) == 0, 1.0, 0.0)
    for p in range(FOX_HEADS // 2):
        qc = main[:, p * LANES:(p + 1) * LANES]
        kc = main[:, FOX_WIDTH + p * LANES:FOX_WIDTH + (p + 1) * LANES]
        vc_t = main[:, 2 * FOX_WIDTH + p * LANES:2 * FOX_WIDTH + (p + 1) * LANES].T
        for hh in range(2):
            h = 2 * p + hh
            hs = slice(h * LANES, (h + 1) * LANES)
            if hh:
                qc = pltpu.roll(qc, FOX_HEAD_DIM, 1)
                kc = pltpu.roll(kc, FOX_HEAD_DIM, 1)
            qa_ref[:, hs] = (qc * q_scale + aug_q[:, hs]).astype(BF16)
            ka_ref[:, hs] = (kc * k_scale + aug_k[:, hs]).astype(BF16)
            vt_ref[h] = jnp.concatenate(
                [vc_t[hh * FOX_HEAD_DIM:(hh + 1) * FOX_HEAD_DIM], ones_row], axis=0).astype(BF16)
    col1 = lax.broadcasted_iota(jnp.int32, (1, 2 * FOX_WIDTH), 1)
    qk = main[:, 0:2 * FOX_WIDTH] * jnp.where(col1 < FOX_WIDTH, LOG2E * FOX_HEAD_DIM ** -0.5, 1.0)
    norms2 = jnp.max(_dot((qk * qk).astype(BF16), seln_ref[...]), axis=0, keepdims=True)
    f2 = cum * LOG2E
    sub = lax.broadcasted_iota(jnp.int32, (SUBLANES, LANES), 0)
    stat_ref[...] = jnp.where(sub == 0, norms2,
                              jnp.where(sub == 1, f2[0:1, :], jnp.where(sub == 2, f2[tm - 1:tm, :], 0.0)))

    o = 3 * FOX_WIDTH
    gq_ref[...] = (main[:, o:o + GLA_KEY_WIDTH] * (GLA_KEY_DIM ** -0.5)).astype(BF16); o += GLA_KEY_WIDTH
    gk_ref[...] = main[:, o:o + GLA_KEY_WIDTH]; o += GLA_KEY_WIDTH
    gv_ref[...] = main[:, o:o + GLA_VAL_WIDTH].astype(BF16); o += GLA_VAL_WIDTH
    go_ref[...] = main[:, o:o + GLA_VAL_WIDTH]

    z = _dot_hi(small, guh_ref[...], gul_ref[...]) + gb_ref[...]
    la_ref[...] = _log_sigmoid(z) * (1.0 / GLA_GATE_TAU)


def _aug_constants():
    selq = np.zeros((LANES, QK_WIDTH), np.float32)
    selk = np.zeros((LANES, QK_WIDTH), np.float32)
    oneq = np.zeros((1, QK_WIDTH), np.float32)
    onek = np.zeros((1, QK_WIDTH), np.float32)
    for h in range(FOX_HEADS):
        base = h * LANES + FOX_HEAD_DIM
        for i, copy in enumerate(F_COPIES):
            selq[copy + h, base + i] = 1.0
            onek[0, base + i] = 1.0
            selk[copy + h, base + 3 + i] = -1.0
            oneq[0, base + 3 + i] = 1.0
    seln = np.zeros((2 * FOX_WIDTH, LANES), np.float32)
    for j in range(2 * FOX_WIDTH):
        seln[j, j // FOX_HEAD_DIM] = 1.0
    return (jnp.asarray(selq, BF16), jnp.asarray(selk, BF16), jnp.asarray(oneq), jnp.asarray(onek),
            jnp.asarray(seln, BF16))


def _in_proj(x2, norm_g, w_main, ws_hi, ws_lo, fb_pad, gu_hi, gu_lo, gb, batch, seq):
    t = batch * seq
    tm = TM_PROJ
    nj = seq // tm
    selq, selk, oneq, onek, seln = _aug_constants()
    rows = lambda w: pl.BlockSpec((tm, w), lambda b, j: (b * nj + j, 0))
    full = lambda a: pl.BlockSpec(a.shape, lambda b, j: (0,) * a.ndim)
    out_shape = [
        jax.ShapeDtypeStruct((t, QK_WIDTH), BF16),
        jax.ShapeDtypeStruct((t, QK_WIDTH), BF16),
        jax.ShapeDtypeStruct((FOX_HEADS, VT_ROWS, t), BF16),
        jax.ShapeDtypeStruct((t, GLA_KEY_WIDTH), BF16),
        jax.ShapeDtypeStruct((t, GLA_KEY_WIDTH), F32),
        jax.ShapeDtypeStruct((t, GLA_VAL_WIDTH), BF16),
        jax.ShapeDtypeStruct((t, GLA_VAL_WIDTH), F32),
        jax.ShapeDtypeStruct((t, GLA_KEY_WIDTH), F32),
        jax.ShapeDtypeStruct((batch * nj * SUBLANES, LANES), F32),
    ]
    out_specs = [rows(QK_WIDTH), rows(QK_WIDTH),
                 pl.BlockSpec((FOX_HEADS, VT_ROWS, tm), lambda b, j: (0, 0, b * nj + j)),
                 rows(GLA_KEY_WIDTH), rows(GLA_KEY_WIDTH), rows(GLA_VAL_WIDTH), rows(GLA_VAL_WIDTH),
                 rows(GLA_KEY_WIDTH),
                 pl.BlockSpec((SUBLANES, LANES), lambda b, j: (b * nj + j, 0))]
    args = (x2, norm_g, w_main, ws_hi, ws_lo, fb_pad, gu_hi, gu_lo, gb, selq, selk, oneq, onek, seln)
    return pl.pallas_call(
        _in_proj_kernel,
        grid=(batch, nj),
        in_specs=[rows(D_MODEL)] + [full(a) for a in args[1:]],
        out_specs=out_specs,
        out_shape=out_shape,
        scratch_shapes=[pltpu.VMEM((1, LANES), F32)],
        compiler_params=pltpu.CompilerParams(
            dimension_semantics=("arbitrary", "arbitrary"), vmem_limit_bytes=VMEM_LIMIT),
        name="in_proj",
    )(*args)


def _fox_kernel(lo_ref, q_ref, k_ref, vt_ref, o_ref, acc_ref, m_ref, sa_ref, ma_ref, sb_ref, mb_ref):
    tq = q_ref.shape[0]
    tk = tq
    nh = q_ref.shape[1] // LANES
    i = pl.program_id(2)

    m_ref[...] = jnp.full_like(m_ref, -jnp.inf)
    acc_ref[...] = jnp.zeros_like(acc_ref)
    slots = ((sa_ref, ma_ref), (sb_ref, mb_ref))

    def stage_scores(block, slot, masked=False):
        start = pl.multiple_of(block * tk, tk)
        for h in range(nh):
            s = _dot_nt(k_ref[pl.ds(start, tk), h * LANES:(h + 1) * LANES],
                        q_ref[:, h * LANES:(h + 1) * LANES])
            if masked:
                key = lax.broadcasted_iota(jnp.int32, (tk, tq), 0)
                qry = lax.broadcasted_iota(jnp.int32, (tk, tq), 1)
                s = jnp.where(key <= qry, s, -jnp.inf)
            slots[slot][0][h] = s
            slots[slot][1][h] = jnp.max(s.reshape(tk // SUBLANES, SUBLANES, tq), axis=0)

    def stage_accumulate(block, slot):
        start = pl.multiple_of(block * tk, tk)
        probs, alphas = [], []
        for h in range(nh):
            m_prev = m_ref[h]
            m_new = jnp.maximum(m_prev, jnp.max(slots[slot][1][h], axis=0, keepdims=True))
            probs.append(jnp.exp2(slots[slot][0][h] - m_new[0:1]).astype(BF16))
            alphas.append(jnp.exp2(m_prev - m_new)[0:1])
            m_ref[h] = m_new
        for h in range(nh):
            acc_ref[h] = alphas[h] * acc_ref[h] + _dot(vt_ref[h, :, pl.ds(start, tk)], probs[h])

    lo = lo_ref[(pl.program_id(0) * pl.num_programs(1) + pl.program_id(1)) * pl.num_programs(2) + i]
    n = i - lo

    def processed(m):
        return jnp.where(m == 0, i, lo + m - 1)

    stage_scores(i, 0, masked=True)
    pairs = n // 2

    def body(mm, c):
        m = 2 * mm
        stage_scores(lo + m, 1)
        stage_accumulate(processed(m), 0)
        stage_scores(lo + m + 1, 0)
        stage_accumulate(lo + m, 1)
        return c

    lax.fori_loop(0, pairs, body, 0)
    last = 2 * pairs

    @pl.when(n > last)
    def _():
        stage_scores(lo + last, 1)
        stage_accumulate(processed(last), 0)
        stage_accumulate(lo + last, 1)

    @pl.when(n == last)
    def _():
        stage_accumulate(processed(last), 0)

    o_t = jnp.concatenate(
        [acc_ref[h, 0:FOX_HEAD_DIM, :] / acc_ref[h, FOX_HEAD_DIM:FOX_HEAD_DIM + 1, :]
         for h in range(nh)], axis=0)
    o_ref[...] = o_t.T.astype(o_ref.dtype)


def _fox(first_block, qa, ka, vt, batch, seq):
    t = batch * seq
    nq = seq // TQ
    nh = FOX_HEADS_PER_STEP
    groups = FOX_HEADS // nh
    grid_spec = pltpu.PrefetchScalarGridSpec(
        num_scalar_prefetch=1,
        grid=(batch, groups, nq),
        in_specs=[
            pl.BlockSpec((TQ, nh * LANES), lambda b, g, i, lo: (b * nq + i, g)),
            pl.BlockSpec((seq, nh * LANES), lambda b, g, i, lo: (b, g)),
            pl.BlockSpec((nh, VT_ROWS, seq), lambda b, g, i, lo: (g, 0, b)),
        ],
        out_specs=pl.BlockSpec((TQ, nh * FOX_HEAD_DIM), lambda b, g, i, lo: (b * nq + i, g)),
        scratch_shapes=[pltpu.VMEM((nh, VT_ROWS, TQ), F32),
                        pltpu.VMEM((nh, SUBLANES, TQ), F32),
                        pltpu.VMEM((nh, TQ, TQ), F32),
                        pltpu.VMEM((nh, SUBLANES, TQ), F32),
                        pltpu.VMEM((nh, TQ, TQ), F32),
                        pltpu.VMEM((nh, SUBLANES, TQ), F32)],
    )
    return pl.pallas_call(
        _fox_kernel,
        grid_spec=grid_spec,
        out_shape=jax.ShapeDtypeStruct((t, FOX_WIDTH), BF16),
        compiler_params=pltpu.CompilerParams(
            dimension_semantics=("parallel", "parallel", "arbitrary"), vmem_limit_bytes=VMEM_LIMIT),
        name="fox_attention",
    )(first_block, qa, ka, vt)


def _gla_kernel(q_ref, k_ref, la_ref, v_ref, go_ref, ng_ref, o_ref, st_ref, obuf_ref):
    rows = q_ref.shape[0]
    pairs = GLA_HEADS // 2

    @pl.when(pl.program_id(1) == 0)
    def _():
        st_ref[...] = jnp.zeros_like(st_ref)

    r = lax.broadcasted_iota(jnp.int32, (CHUNK, CHUNK), 0)
    c = lax.broadcasted_iota(jnp.int32, (CHUNK, CHUNK), 1)
    tri = jnp.where(r >= c, 1.0, 0.0).astype(BF16)
    vrow = lax.broadcasted_iota(jnp.int32, (2 * GLA_VAL_DIM, 2 * GLA_KEY_DIM), 0)
    kcol = lax.broadcasted_iota(jnp.int32, (2 * GLA_VAL_DIM, 2 * GLA_KEY_DIM), 1)
    same_head = (vrow >= GLA_VAL_DIM) == (kcol >= GLA_KEY_DIM)

    for ch in range(rows // CHUNK):
        sl = slice(ch * CHUNK, (ch + 1) * CHUNK)
        for p in range(pairs):
            ks = slice(p * 2 * GLA_KEY_DIM, (p + 1) * 2 * GLA_KEY_DIM)
            vs = slice(p * 2 * GLA_VAL_DIM, (p + 1) * 2 * GLA_VAL_DIM)
            cum = _dot3_exact_lhs(tri, la_ref[sl, ks])
            tot = cum[CHUNK - 1:CHUNK, :]
            kd = (k_ref[sl, ks] * jnp.exp(tot - cum)).astype(BF16)
            upd_t = _dot_tn(v_ref[sl, vs], kd)
            st = st_ref[p] * jnp.exp(tot) + jnp.where(same_head, upd_t, 0.0)
            st_ref[p] = st
            obuf_ref[sl, vs] = _dot_nt(q_ref[sl, ks], st.astype(BF16))

    o = obuf_ref[...]
    go = go_ref[...]
    ng = ng_ref[...]
    halves = []
    for h in range(GLA_HEADS):
        hs = slice(h * GLA_VAL_DIM, (h + 1) * GLA_VAL_DIM)
        oh = o[:, hs]
        oh = oh * lax.rsqrt(jnp.mean(oh * oh, axis=-1, keepdims=True) + EPS) * ng
        gh = go[:, hs]
        halves.append(oh * (gh * jax.nn.sigmoid(gh)))
    o_ref[...] = jnp.concatenate(halves, axis=1).astype(o_ref.dtype)


def _gla(gq, gk, la, gv, go, ng, batch, seq):
    t = batch * seq
    nr = seq // GLA_ROWS
    kspec = pl.BlockSpec((GLA_ROWS, GLA_KEY_WIDTH), lambda b, i: (b * nr + i, 0))
    vspec = pl.BlockSpec((GLA_ROWS, GLA_VAL_WIDTH), lambda b, i: (b * nr + i, 0))
    return pl.pallas_call(
        _gla_kernel,
        grid=(batch, nr),
        in_specs=[kspec, kspec, kspec, vspec, vspec,
                  pl.BlockSpec((1, GLA_VAL_DIM), lambda b, i: (0, 0))],
        out_specs=vspec,
        out_shape=jax.ShapeDtypeStruct((t, GLA_VAL_WIDTH), BF16),
        scratch_shapes=[pltpu.VMEM((GLA_HEADS // 2, 2 * GLA_VAL_DIM, 2 * GLA_KEY_DIM), F32),
                        pltpu.VMEM((GLA_ROWS, GLA_VAL_WIDTH), F32)],
        compiler_params=pltpu.CompilerParams(
            dimension_semantics=("parallel", "arbitrary"), vmem_limit_bytes=VMEM_LIMIT),
        name="gla",
    )(gq, gk, la, gv, go, ng)


def _out_proj_kernel(x_ref, fox_ref, gla_ref, wf_ref, wg_ref, g2_ref, rwh_ref, rwl_ref, rb_ref,
                     h_ref, n2_ref, ti_ref, gate_ref, rank_ref, cnt_ref, carry_ref):
    tm = x_ref.shape[0]

    @pl.when(pl.program_id(0) == 0)
    def _():
        carry_ref[...] = jnp.zeros_like(carry_ref)

    h = x_ref[...] + _dot(fox_ref[...], wf_ref[...]) + _dot(gla_ref[...], wg_ref[...])
    h_ref[...] = h
    n2 = _rms(h, g2_ref[...])
    for c in range(ROW_TILES):
        n2_ref[pl.ds(c, tm, stride=ROW_TILES), :] = n2[:, c * LANES:(c + 1) * LANES]

    logits = _dot_hi(n2, rwh_ref[...], rwl_ref[...]) + rb_ref[...]
    work = logits.T[0:N_EXPERTS, :]
    eidx = lax.broadcasted_iota(jnp.int32, (N_EXPERTS, tm), 0)
    vals, idxs, onehots = [], [], []
    for _ in range(TOP_K):
        m = jnp.max(work, axis=0, keepdims=True)
        idx = jnp.min(jnp.where(work == m, eidx, N_EXPERTS), axis=0, keepdims=True)
        oh = eidx == idx
        vals.append(m)
        idxs.append(idx)
        onehots.append(oh)
        work = jnp.where(oh, -jnp.inf, work)
    exps = [jnp.exp(v - vals[0]) for v in vals]
    denom = exps[0] + exps[1] + exps[2] + exps[3]

    chosen = jnp.zeros((N_EXPERTS, tm), F32)
    for oh in onehots:
        chosen = chosen + jnp.where(oh, 1.0, 0.0)
    chosen = chosen.astype(BF16)
    row = lax.broadcasted_iota(jnp.int32, (tm, tm), 0)
    col = lax.broadcasted_iota(jnp.int32, (tm, tm), 1)
    earlier = jnp.where(row < col, 1.0, 0.0).astype(BF16)
    carry = carry_ref[...]
    before = _dot(chosen, earlier) + carry
    carry = carry + _dot(chosen, jnp.ones((tm, tm), BF16))
    carry_ref[...] = carry
    cnt_ref[...] = carry[:, 0:LANES].astype(jnp.int32)

    sub = lax.broadcasted_iota(jnp.int32, (SUBLANES, tm), 0)
    ti = jnp.zeros((SUBLANES, tm), jnp.int32)
    gates = jnp.zeros((SUBLANES, tm), F32)
    ranks = jnp.zeros((SUBLANES, tm), F32)
    for kk in range(TOP_K):
        sel = sub == kk
        ti = jnp.where(sel, idxs[kk], ti)
        gates = jnp.where(sel, exps[kk] / denom, gates)
        rk = jnp.sum(jnp.where(onehots[kk], before, 0.0), axis=0, keepdims=True)
        ranks = jnp.where(sel, rk, ranks)
    ti_ref[...] = ti
    gate_ref[...] = gates
    rank_ref[...] = ranks.astype(jnp.int32)


def _out_proj(x2, fox, gla, wf, wg, g2, rw_hi, rw_lo, rb_pad):
    t = x2.shape[0]
    tm = TM_PROJ
    rows = lambda w: pl.BlockSpec((tm, w), lambda i: (i, 0))
    full = lambda a: pl.BlockSpec(a.shape, lambda i: (0,) * a.ndim)
    out_shape = [
        jax.ShapeDtypeStruct((t, D_MODEL), F32),
        jax.ShapeDtypeStruct((t * ROW_TILES, LANES), F32),
        jax.ShapeDtypeStruct((SUBLANES, t), jnp.int32),
        jax.ShapeDtypeStruct((SUBLANES, t), F32),
        jax.ShapeDtypeStruct((SUBLANES, t), jnp.int32),
        jax.ShapeDtypeStruct((N_EXPERTS, LANES), jnp.int32),
    ]
    per_token = pl.BlockSpec((SUBLANES, tm), lambda i: (0, i))
    out_specs = [rows(D_MODEL), pl.BlockSpec((tm * ROW_TILES, LANES), lambda i: (i, 0)),
                 per_token, per_token, per_token,
                 pl.BlockSpec((N_EXPERTS, LANES), lambda i: (0, 0))]
    return pl.pallas_call(
        _out_proj_kernel,
        grid=(t // tm,),
        in_specs=[rows(D_MODEL), rows(FOX_WIDTH), rows(GLA_VAL_WIDTH), full(wf), full(wg), full(g2),
                  full(rw_hi), full(rw_lo), full(rb_pad)],
        out_specs=out_specs,
        out_shape=out_shape,
        scratch_shapes=[pltpu.VMEM((N_EXPERTS, tm), F32)],
        compiler_params=pltpu.CompilerParams(
            dimension_semantics=("arbitrary",), vmem_limit_bytes=VMEM_LIMIT),
        name="out_proj_router",
    )(x2, fox, gla, wf, wg, g2, rw_hi, rw_lo, rb_pad)


def _slot_sources_kernel(starts_ref, cnt_ref, ends_ref, pos_ref, src_ref):
    g = pl.program_id(0)
    nblk = pos_ref.shape[0]
    n_slots = src_ref.shape[0]
    n_pairs = nblk * pl.num_programs(0)

    @pl.when(g == 0)
    def _():
        def fill(lo, hi, dump):
            def body(r, d):
                src_ref[r] = d
                return d + 1
            return lax.fori_loop(lo, hi, body, dump)

        def per_expert(e, dump):
            return fill(starts_ref[e] + cnt_ref[e], ends_ref[e], dump)

        dump = lax.fori_loop(0, N_EXPERTS, per_expert, jnp.int32(n_pairs))
        fill(ends_ref[N_EXPERTS - 1], n_slots, dump)

    base = g * nblk

    def body(j, c):
        for u in range(SCALAR_UNROLL):
            p = j * SCALAR_UNROLL + u
            src_ref[pos_ref[p]] = base + p
        return c

    lax.fori_loop(0, nblk // SCALAR_UNROLL, body, 0)


def _slot_sources(starts, cnt, ends, pos_km, n_slots):
    n_pairs = pos_km.shape[0]
    blk = SLOT_BLOCK
    grid_spec = pltpu.PrefetchScalarGridSpec(
        num_scalar_prefetch=3,
        grid=(n_pairs // blk,),
        in_specs=[pl.BlockSpec((blk,), lambda g, s, c, e: (g,), memory_space=pltpu.SMEM)],
        out_specs=pl.BlockSpec((n_slots,), lambda g, s, c, e: (0,), memory_space=pltpu.SMEM),
    )
    return pl.pallas_call(
        _slot_sources_kernel,
        grid_spec=grid_spec,
        out_shape=jax.ShapeDtypeStruct((n_slots,), jnp.int32),
        compiler_params=pltpu.CompilerParams(dimension_semantics=("arbitrary",)),
        name="slot_sources",
    )(starts, cnt, ends, pos_km)


def _tile_rows(ref, row):
    return ref.at[pl.ds(pl.multiple_of(row * ROW_TILES, ROW_TILES), ROW_TILES)]


def _experts_kernel(te_ref, nu_ref, src_ref, first_ref, par_ref, nxt_ref,
                    n2_ref, wi_ref, bi0_ref, bi1_ref, wo_ref, bo0_ref, bo1_ref, yg_ref,
                    wib_ref, wob_ref, wif_ref, wof_ref, xbuf_ref, ybuf_ref, gsem, ssem, wsem):
    step = pl.program_id(0)
    n_tiles = 2 * pl.num_programs(0)
    tm = xbuf_ref.shape[1] // ROW_TILES
    n_tokens = n2_ref.shape[0] // ROW_TILES
    nu = nu_ref[0]

    def issue_gather(tile, slot):
        base = jnp.minimum(tile, n_tiles - 1) * tm
        for r in range(tm):
            token = src_ref[base + r] & (n_tokens - 1)
            pltpu.make_async_copy(
                _tile_rows(n2_ref, token), xbuf_ref.at[slot, pl.ds(r * ROW_TILES, ROW_TILES)],
                gsem.at[slot]).start(priority=r % 2)

    def wait_gather(slot):
        pltpu.make_async_copy(
            n2_ref.at[pl.ds(0, tm * ROW_TILES)], xbuf_ref.at[slot], gsem.at[slot]).wait()

    def issue_scatter(tile, slot):
        base = tile * tm
        for r in range(tm):
            pltpu.make_async_copy(
                ybuf_ref.at[slot, pl.ds(r * ROW_TILES, ROW_TILES)],
                _tile_rows(yg_ref, src_ref[base + r]), ssem.at[slot]).start(priority=r % 2)

    def wait_scatter(slot):
        pltpu.make_async_copy(
            ybuf_ref.at[slot], yg_ref.at[pl.ds(0, tm * ROW_TILES)], ssem.at[slot]).wait()

    def weight_copies(expert, slot):
        return (pltpu.make_async_copy(wi_ref.at[expert], wif_ref.at[slot], wsem.at[slot, 0]),
                pltpu.make_async_copy(wo_ref.at[expert], wof_ref.at[slot], wsem.at[slot, 1]))

    @pl.when(step == 0)
    def _():
        issue_gather(0, 0)
        issue_gather(1, 1)
        ybuf_ref[1] = jnp.zeros_like(ybuf_ref[1])
        for c in weight_copies(te_ref[0], 0):
            c.start()

    def run_tile(i, slot, bi_ref, bo_ref):
        @pl.when(first_ref[i] == 1)
        def _():
            wslot = par_ref[i]

            @pl.when(nxt_ref[i] >= 0)
            def _():
                for c in weight_copies(nxt_ref[i], 1 - wslot):
                    c.start()

            for c in weight_copies(te_ref[i], wslot):
                c.wait()
            wib_ref[...] = wif_ref[wslot].astype(BF16)
            wob_ref[...] = wof_ref[wslot].astype(BF16)

        @pl.when((i >= 1) & (i < nu))
        def _():
            wait_scatter(slot)

        @pl.when(i < nu)
        def _():
            wait_gather(slot)
            x = jnp.concatenate(
                [xbuf_ref[slot, pl.ds(c, tm, stride=ROW_TILES), :] for c in range(ROW_TILES)],
                axis=1).astype(BF16)
            issue_gather(i + 2, slot)
            issue_scatter(jnp.maximum(i - 1, 0), 1 - slot)
            h = _dot(x, wib_ref[...]) + bi_ref[0]
            gate = jnp.minimum(h[:, :D_FF], SWIGLU_LIMIT)
            lin = jnp.clip(h[:, D_FF:], -SWIGLU_LIMIT, SWIGLU_LIMIT)
            a = (lin + 1.0) * (gate * jax.nn.sigmoid(SWIGLU_ALPHA * gate))
            y = _dot(a.astype(BF16), wob_ref[...]) + bo_ref[0]
            for c in range(ROW_TILES):
                ybuf_ref[slot, pl.ds(c, tm, stride=ROW_TILES), :] = y[:, c * LANES:(c + 1) * LANES]

        @pl.when(i == nu - 1)
        def _():
            wait_scatter(1 - slot)
            issue_scatter(i, slot)
            wait_gather(1 - slot)
            wait_gather(slot)
            wait_scatter(slot)

        @pl.when(i >= nu)
        def _():
            ybuf_ref[slot] = jnp.zeros_like(ybuf_ref[slot])
            tail = pltpu.make_async_copy(
                ybuf_ref.at[slot],
                yg_ref.at[pl.ds(pl.multiple_of(src_ref[i * tm] * ROW_TILES, ROW_TILES), tm * ROW_TILES)],
                ssem.at[slot])
            tail.start()
            tail.wait()

    run_tile(2 * step, 0, bi0_ref, bo0_ref)
    run_tile(2 * step + 1, 1, bi1_ref, bo1_ref)


def _experts(tile_expert, num_used, src, group_first, group_parity, next_expert,
             n2_tiles, w_in, b_in, w_out, b_out):
    n_tiles = tile_expert.shape[0]
    assert n_tiles % 2 == 0
    tm = TM_EXP
    bias = lambda w, k: pl.BlockSpec((1, 1, w), lambda s, te, *_: (te[2 * s + k], 0, 0))
    grid_spec = pltpu.PrefetchScalarGridSpec(
        num_scalar_prefetch=6,
        grid=(n_tiles // 2,),
        in_specs=[pl.BlockSpec(memory_space=pl.ANY), pl.BlockSpec(memory_space=pl.ANY),
                  bias(2 * D_FF, 0), bias(2 * D_FF, 1),
                  pl.BlockSpec(memory_space=pl.ANY), bias(D_MODEL, 0), bias(D_MODEL, 1)],
        out_specs=pl.BlockSpec(memory_space=pl.ANY),
        scratch_shapes=[pltpu.VMEM((D_MODEL, 2 * D_FF), BF16), pltpu.VMEM((D_FF, D_MODEL), BF16),
                        pltpu.VMEM((2, D_MODEL, 2 * D_FF), F32),
                        pltpu.VMEM((2, D_FF, D_MODEL), F32),
                        pltpu.VMEM((2, tm * ROW_TILES, LANES), F32),
                        pltpu.VMEM((2, tm * ROW_TILES, LANES), F32),
                        pltpu.SemaphoreType.DMA((2,)), pltpu.SemaphoreType.DMA((2,)),
                        pltpu.SemaphoreType.DMA((2, 2))],
    )
    return pl.pallas_call(
        _experts_kernel,
        grid_spec=grid_spec,
        out_shape=jax.ShapeDtypeStruct((n_tiles * tm * ROW_TILES, LANES), F32),
        compiler_params=pltpu.CompilerParams(
            dimension_semantics=("arbitrary",), vmem_limit_bytes=VMEM_LIMIT),
        name="experts",
    )(tile_expert, num_used, src, group_first, group_parity, next_expert,
      n2_tiles, w_in, b_in, b_in, w_out, b_out, b_out)


def _combine_kernel(gate_ref, h_ref, fg_ref, y0_ref, y1_ref, y2_ref, y3_ref, o_ref):
    tc = h_ref.shape[0]
    gates = gate_ref[...].T
    h = h_ref[...]
    for kk, y_ref in enumerate((y0_ref, y1_ref, y2_ref, y3_ref)):
        yk = jnp.concatenate(
            [y_ref[pl.ds(c, tc, stride=ROW_TILES), :] for c in range(ROW_TILES)], axis=1)
        h = h + gates[:, kk:kk + 1] * yk
    o_ref[...] = _rms(h, fg_ref[...])


def _combine(gates, h1, final_g, yg):
    t = h1.shape[0]
    tc = TC_COMBINE
    nblk = t // tc

    def pair_rows(kk):
        return pl.BlockSpec((tc * ROW_TILES, LANES), lambda i: (kk * nblk + i, 0))

    return pl.pallas_call(
        _combine_kernel,
        grid=(nblk,),
        in_specs=[pl.BlockSpec((SUBLANES, tc), lambda i: (0, i)),
                  pl.BlockSpec((tc, D_MODEL), lambda i: (i, 0)),
                  pl.BlockSpec((1, D_MODEL), lambda i: (0, 0))] + [pair_rows(kk) for kk in range(TOP_K)],
        out_specs=pl.BlockSpec((tc, D_MODEL), lambda i: (i, 0)),
        out_shape=jax.ShapeDtypeStruct((t, D_MODEL), F32),
        compiler_params=pltpu.CompilerParams(
            dimension_semantics=("parallel",), vmem_limit_bytes=VMEM_LIMIT),
        name="combine",
    )(gates, h1, final_g, yg, yg, yg, yg)


def _hi_lo(w):
    hi = w.astype(BF16)
    return hi, (w - hi.astype(F32)).astype(BF16)


def _first_needed_block(stats, batch, seq):
    assert TM_PROJ == TQ
    nq = seq // TQ
    st = stats.reshape(batch, nq, SUBLANES, LANES)
    q_norm = jnp.sqrt(st[:, :, 0, 0:FOX_HEADS])
    k_norm = jnp.sqrt(jnp.max(st[:, :, 0, FOX_HEADS:2 * FOX_HEADS], axis=1))
    f_first = st[:, :, 1, 0:FOX_HEADS]
    f_last = st[:, :, 2, 0:FOX_HEADS]
    bound = SKIP_MARGIN + 2.0 * NORM_SLACK * q_norm * k_norm[:, None, :]
    decay = f_first[:, :, None, :] - f_last[:, None, :, :]
    tile = jnp.arange(nq, dtype=jnp.int32)
    needed = (tile[None, :] < tile[:, None])[None, :, :, None] & (decay >= -bound[:, :, None, :])
    lo = jnp.min(jnp.where(needed, tile[None, None, :, None], tile[None, :, None, None]), axis=2)
    lo = jnp.min(lo.reshape(batch, nq, FOX_HEADS // FOX_HEADS_PER_STEP, FOX_HEADS_PER_STEP), axis=-1)
    return lo.transpose(0, 2, 1).reshape(-1).astype(jnp.int32)


def _layer(x2, batch, seq, norm1_g, w_in, fox_f_bias, gla_gate_up, gla_gate_bias, gla_norm_g, w_out,
           norm2_g, router_w, router_b, exp_w_in, exp_b_in, exp_w_out, exp_b_out, final_g):
    t = batch * seq
    o = 0
    segs = {}
    for name, width in (("fq", FOX_WIDTH), ("fk", FOX_WIDTH), ("fv", FOX_WIDTH), ("ff", FOX_HEADS),
                        ("gq", GLA_KEY_WIDTH), ("gk", GLA_KEY_WIDTH), ("gv", GLA_VAL_WIDTH),
                        ("gl", GLA_GATE_RANK), ("go", GLA_VAL_WIDTH)):
        segs[name] = w_in[:, o:o + width]
        o += width
    w_main = jnp.concatenate(
        [segs[n] for n in ("fq", "fk", "fv", "gq", "gk", "gv", "go")], axis=1).astype(BF16)
    n_small = FOX_HEADS + GLA_GATE_RANK
    w_small = jnp.concatenate(
        [segs["ff"], segs["gl"], segs["ff"], segs["ff"],
         jnp.zeros((D_MODEL, LANES - n_small - 2 * FOX_HEADS), F32)], axis=1)
    ws_hi, ws_lo = _hi_lo(w_small)
    fb_pad = jnp.concatenate(
        [fox_f_bias, jnp.zeros((GLA_GATE_RANK,), F32), fox_f_bias, fox_f_bias,
         jnp.zeros((LANES - n_small - 2 * FOX_HEADS,), F32)]).reshape(1, LANES)
    gu_pad = jnp.pad(gla_gate_up, ((FOX_HEADS, LANES - n_small), (0, 0)))
    gu_hi, gu_lo = _hi_lo(gu_pad)

    qa, ka, vt, gq, gk, gv, go, la, stats = _in_proj(
        x2, norm1_g.reshape(1, D_MODEL), w_main, ws_hi, ws_lo, fb_pad, gu_hi, gu_lo,
        gla_gate_bias.reshape(1, GLA_KEY_WIDTH), batch, seq)

    fox = _fox(_first_needed_block(stats, batch, seq), qa, ka, vt, batch, seq)
    gla = _gla(gq, gk, la, gv, go, gla_norm_g.reshape(1, GLA_VAL_DIM), batch, seq)

    w_out_b = w_out.astype(BF16)
    rw_hi, rw_lo = _hi_lo(jnp.pad(router_w, ((0, 0), (0, LANES - N_EXPERTS))))
    rb_pad = jnp.pad(router_b, (0, LANES - N_EXPERTS), constant_values=NEG_BIG).reshape(1, LANES)
    h1, n2_tiles, ti, gates, rank, counts = _out_proj(
        x2, fox, gla, w_out_b[:FOX_WIDTH], w_out_b[FOX_WIDTH:], norm2_g.reshape(1, D_MODEL),
        rw_hi, rw_lo, rb_pad)

    cnt = counts[:, 0]
    padded = ((cnt + TM_EXP - 1) // TM_EXP) * TM_EXP
    ends = jnp.cumsum(padded)
    starts = ends - padded
    expert_ids = jnp.arange(N_EXPERTS, dtype=jnp.int32)
    start_of = jnp.sum(jnp.where(ti[:TOP_K, :, None] == expert_ids, starts, 0), axis=-1)
    pos_km = (start_of + rank[:TOP_K]).reshape(-1).astype(jnp.int32)
    n_tiles = (t * TOP_K) // TM_EXP + N_EXPERTS
    num_used = (ends[-1] // TM_EXP).astype(jnp.int32)
    tile_start = jnp.arange(n_tiles, dtype=jnp.int32) * TM_EXP
    tile_expert = jnp.minimum(
        jnp.sum((tile_start[:, None] >= ends[None, :]).astype(jnp.int32), axis=1), N_EXPERTS - 1)
    last_expert = tile_expert[jnp.maximum(num_used - 1, 0)]
    tile_expert = jnp.where(jnp.arange(n_tiles) < num_used, tile_expert, last_expert).astype(jnp.int32)

    src = _slot_sources(starts.astype(jnp.int32), cnt.astype(jnp.int32), ends.astype(jnp.int32), pos_km,
                        n_tiles * TM_EXP)
    tile_ids = jnp.arange(n_tiles, dtype=jnp.int32)
    group_first = jnp.concatenate(
        [jnp.ones((1,), jnp.int32), (tile_expert[1:] != tile_expert[:-1]).astype(jnp.int32)])
    group_parity = (jnp.cumsum(group_first) - 1) & 1
    first_pos = jnp.where(group_first == 1, tile_ids, n_tiles)
    next_first = jnp.min(
        jnp.where(first_pos[None, :] > tile_ids[:, None], first_pos[None, :], n_tiles), axis=1)
    next_expert = jnp.where(next_first < n_tiles,
                            tile_expert[jnp.minimum(next_first, n_tiles - 1)], -1).astype(jnp.int32)
    yg = _experts(tile_expert, num_used.reshape(1), src, group_first, group_parity.astype(jnp.int32),
                  next_expert, n2_tiles, exp_w_in,
                  exp_b_in.reshape(N_EXPERTS, 1, 2 * D_FF), exp_w_out,
                  exp_b_out.reshape(N_EXPERTS, 1, D_MODEL))
    return _combine(gates, h1, final_g.reshape(1, D_MODEL), yg)


def kernel(x, norm1_g, w_in, fox_f_bias, gla_gate_up, gla_gate_bias, gla_norm_g, w_out, norm2_g,
           router_w, router_b, exp_w_in, exp_b_in, exp_w_out, exp_b_out, final_g):
    batch, seq, d = x.shape
    depth = norm1_g.shape[0]
    assert depth == 1 and d == D_MODEL
    out = _layer(x.reshape(batch * seq, d), batch, seq, norm1_g[0], w_in[0], fox_f_bias[0],
                 gla_gate_up[0], gla_gate_bias[0], gla_norm_g[0], w_out[0], norm2_g[0], router_w[0],
                 router_b[0], exp_w_in[0], exp_b_in[0], exp_w_out[0], exp_b_out[0], final_g)
    return out.reshape(batch, seq, d)
```
